```python
import math
import jax, jax.numpy as jnp
from jax import lax
import numpy as np

D_MODEL = 2048
BATCH = 4
SEQ = 8192
DEPTH = 1

D_MIX = D_MODEL
SB_HEADS = 8
SB_HEAD_DIM = 128
D_ATTN = SB_HEADS * SB_HEAD_DIM
D_CONV = D_MIX - D_ATTN
CONV_WIDTH = 3
D_FF = 5632
Q_BLOCK = 128
FFN_RES_WEIGHT = 0.5
EPS = 1e-6
D_IN_PROJ = 3 * D_ATTN + 3 * D_CONV

kernel_name = "hybrid_stickbreak_shortconv_macaron"


def rms_norm(x, g):
    xf = x.astype(jnp.float32)
    y = xf * lax.rsqrt(jnp.mean(xf * xf, axis=-1, keepdims=True) + EPS)
    return (y * g.astype(jnp.float32)).astype(x.dtype)


def swiglu(x, w_gate, w_up, w_down):
    return (jax.nn.silu(x @ w_gate) * (x @ w_up)) @ w_down


def stick_breaking_attention(q, k, v):
    b, h, s, dh = q.shape
    nblk = s // Q_BLOCK
    scale = 1.0 / math.sqrt(dh)
    qb = q.reshape(b, h, nblk, Q_BLOCK, dh).transpose(2, 0, 1, 3, 4)
    kpos = jnp.arange(s)
    vf = v.astype(jnp.float32)

    def one_block(args):
        qi, bi = args
        z = jnp.einsum('bhqd,bhkd->bhqk', qi, k).astype(jnp.float32) * scale
        qpos = bi * Q_BLOCK + jnp.arange(Q_BLOCK)
        valid = kpos[None, :] < qpos[:, None]
        log_1m_beta = jnp.where(valid, -jax.nn.softplus(z), 0.0)
        tail = lax.cumsum(log_1m_beta, axis=3, reverse=True) - log_1m_beta
        a = jnp.where(valid, jnp.exp(jax.nn.log_sigmoid(z) + tail), 0.0)
        return jnp.einsum('bhqk,bhkd->bhqd', a, vf)

    out = lax.map(one_block, (qb, jnp.arange(nblk)))
    return out.transpose(1, 2, 0, 3, 4).reshape(b, h, s, dh).astype(q.dtype)


def causal_depthwise_conv(u, w, bias):
    s = u.shape[1]
    up = jnp.pad(u, ((0, 0), (CONV_WIDTH - 1, 0), (0, 0)))
    y = bias
    for i in range(CONV_WIDTH):
        y = y + up[:, i:i + s, :] * w[i]
    return y


def setup_inputs(seed: int = 0) -> dict:
    key = jax.random.key(seed)
    ks = jax.random.split(key, 20)

    def nrm(k, shape, fan_in):
        return jax.random.normal(k, shape, jnp.float32) * (fan_in ** -0.5)

    def gain(k, shape):
        return 1.0 + 0.01 * jax.random.normal(k, shape, jnp.float32)

    L = DEPTH
    return {
        "x": jax.random.normal(ks[0], (BATCH, SEQ, D_MODEL), jnp.float32),
        "ffn1_norm": gain(ks[1], (L, D_MODEL)),
        "ffn1_w_gate": nrm(ks[2], (L, D_MODEL, D_FF), D_MODEL),
        "ffn1_w_up": nrm(ks[3], (L, D_MODEL, D_FF), D_MODEL),
        "ffn1_w_down": nrm(ks[4], (L, D_FF, D_MODEL), D_FF),
        "mix_norm": gain(ks[5], (L, D_MODEL)),
        "w_in": nrm(ks[6], (L, D_MODEL, D_IN_PROJ), D_MODEL),
        "q_norm": gain(ks[7], (L, SB_HEAD_DIM)),
        "k_norm": gain(ks[8], (L, SB_HEAD_DIM)),
        "conv_w": nrm(ks[9], (L, CONV_WIDTH, D_CONV), CONV_WIDTH),
        "conv_b": 0.01 * jax.random.normal(ks[10], (L, D_CONV), jnp.float32),
        "attn_out_norm": gain(ks[11], (L, D_ATTN)),
        "conv_out_norm": gain(ks[12], (L, D_CONV)),
        "w_out": nrm(ks[13], (L, D_MIX, D_MODEL), D_MIX),
        "ffn2_norm": gain(ks[14], (L, D_MODEL)),
        "ffn2_w_gate": nrm(ks[15], (L, D_MODEL, D_FF), D_MODEL),
        "ffn2_w_up": nrm(ks[16], (L, D_MODEL, D_FF), D_MODEL),
        "ffn2_w_down": nrm(ks[17], (L, D_FF, D_MODEL), D_FF),
    }


def reference(x, ffn1_norm, ffn1_w_gate, ffn1_w_up, ffn1_w_down, mix_norm, w_in,
              q_norm, k_norm, conv_w, conv_b, attn_out_norm, conv_out_norm, w_out,
              ffn2_norm, ffn2_w_gate, ffn2_w_up, ffn2_w_down):
    b, s, _ = x.shape
    for l in range(DEPTH):
        x = x + FFN_RES_WEIGHT * swiglu(rms_norm(x, ffn1_norm[l]), ffn1_w_gate[l], ffn1_w_up[l], ffn1_w_down[l])

        h = rms_norm(x, mix_norm[l])
        proj = h @ w_in[l]
        q, k, v, gb, gc, u = jnp.split(proj, [D_ATTN, 2 * D_ATTN, 3 * D_ATTN,
                                              3 * D_ATTN + D_CONV, 3 * D_ATTN + 2 * D_CONV], axis=-1)

        q = rms_norm(q.reshape(b, s, SB_HEADS, SB_HEAD_DIM), q_norm[l]).transpose(0, 2, 1, 3)
        k = rms_norm(k.reshape(b, s, SB_HEADS, SB_HEAD_DIM), k_norm[l]).transpose(0, 2, 1, 3)
        v = v.reshape(b, s, SB_HEADS, SB_HEAD_DIM).transpose(0, 2, 1, 3)
        y_attn = stick_breaking_attention(q, k, v).transpose(0, 2, 1, 3).reshape(b, s, D_ATTN)
        y_attn = rms_norm(y_attn, attn_out_norm[l])

        y_conv = gb * causal_depthwise_conv(gc * u, conv_w[l], conv_b[l])
        y_conv = rms_norm(y_conv, conv_out_norm[l])

        x = x + jnp.concatenate([y_attn, y_conv], axis=-1) @ w_out[l]

        x = x + FFN_RES_WEIGHT * swiglu(rms_norm(x, ffn2_norm[l]), ffn2_w_gate[l], ffn2_w_up[l], ffn2_w_down[l])
    return x
```

```python
import functools
import math

import jax
import jax.numpy as jnp
from jax import lax
from jax.experimental import pallas as pl
from jax.experimental.pallas import tpu as pltpu

F32 = jnp.float32
BF16 = jnp.bfloat16

EPS = 1e-6
FFN_RES_WEIGHT = 0.5
HEAD_DIM = 128
CONV_WIDTH = 3
KEY_BLOCK = 128
CONV_HALO_ROWS = 8

STICK_EXHAUSTED = 110.0

V7X_VMEM_LIMIT_BYTES = 56 * 1024 * 1024


def _rms_norm(x, gain):
    ms = jnp.mean(x * x, axis=-1, keepdims=True)
    return x * lax.rsqrt(ms + EPS) * gain


def _ffn_kernel(x_ref, g_ref, wg_ref, wu_ref, wd_ref, o_ref, xn_ref, *, n_chunk):
    j = pl.program_id(1)

    @pl.when(j == 0)
    def _():
        x = x_ref[...]
        xn_ref[...] = _rms_norm(x, g_ref[...]).astype(BF16)
        o_ref[...] = x

    xn = xn_ref[...]
    hg = jnp.dot(xn, wg_ref[...], preferred_element_type=F32)
    hu = jnp.dot(xn, wu_ref[...], preferred_element_type=F32)
    act = (hg * jax.nn.sigmoid(hg) * hu * FFN_RES_WEIGHT).astype(BF16)
    d_model = o_ref.shape[1]
    for n in range(0, d_model, n_chunk):
        o_ref[:, n:n + n_chunk] += jnp.dot(
            act, wd_ref[:, n:n + n_chunk], preferred_element_type=F32)


def _ffn(x, gain, wg, wu, wd, *, tm, tf, n_chunk):
    n_tok, d_model = x.shape
    d_ff = wg.shape[1]
    grid = (n_tok // tm, d_ff // tf)
    return pl.pallas_call(
        functools.partial(_ffn_kernel, n_chunk=n_chunk),
        grid=grid,
        in_specs=[
            pl.BlockSpec((tm, d_model), lambda i, j: (i, 0)),
            pl.BlockSpec((1, d_model), lambda i, j: (0, 0)),
            pl.BlockSpec((d_model, tf), lambda i, j: (0, j)),
            pl.BlockSpec((d_model, tf), lambda i, j: (0, j)),
            pl.BlockSpec((tf, d_model), lambda i, j: (j, 0)),
        ],
        out_specs=pl.BlockSpec((tm, d_model), lambda i, j: (i, 0)),
        out_shape=jax.ShapeDtypeStruct((n_tok, d_model), F32),
        scratch_shapes=[pltpu.VMEM((tm, d_model), BF16)],
        compiler_params=pltpu.CompilerParams(
            dimension_semantics=("parallel", "arbitrary"),
            vmem_limit_bytes=V7X_VMEM_LIMIT_BYTES),
        name="ffn",
    )(x, gain, wg, wu, wd)


def _in_proj_kernel(x_ref, g_ref, wa_ref, wu_ref, qg_ref, kg_ref,
                    q_ref, k_ref, v_ref, gb_ref, p_ref, xn_ref):
    c = pl.program_id(1)

    @pl.when(c == 0)
    def _():
        xn_ref[...] = _rms_norm(x_ref[...], g_ref[...]).astype(BF16)

    xn = xn_ref[...]
    a = jnp.dot(xn, wa_ref[...], preferred_element_type=F32)

    def head_norm(gain, out_ref):
        for h in range(a.shape[1] // HEAD_DIM):
            cols = slice(h * HEAD_DIM, (h + 1) * HEAD_DIM)
            out_ref[:, cols] = _rms_norm(a[:, cols], gain).astype(BF16)

    @pl.when(c == 0)
    def _():
        head_norm(qg_ref[...], q_ref)

    @pl.when(c == 1)
    def _():
        head_norm(kg_ref[...], k_ref)

    @pl.when(c == 2)
    def _():
        v_ref[...] = a.astype(BF16)

    @pl.when(c == 3)
    def _():
        gb_ref[...] = a.astype(BF16)

    @pl.when(c == 4)
    def _():
        u = jnp.dot(xn, wu_ref[...], preferred_element_type=F32)
        p_ref[...] = (a * u).astype(BF16)


def _in_proj(x, gain, w_in, q_gain, k_gain, *, tm):
    n_tok, d_model = x.shape
    d_sec = w_in.shape[1] // 6
    sec_out = pl.BlockSpec((tm, d_sec), lambda i, c: (i, 0))
    out_sds = jax.ShapeDtypeStruct((n_tok, d_sec), BF16)
    return pl.pallas_call(
        _in_proj_kernel,
        grid=(n_tok // tm, 5),
        in_specs=[
            pl.BlockSpec((tm, d_model), lambda i, c: (i, 0)),
            pl.BlockSpec((1, d_model), lambda i, c: (0, 0)),
            pl.BlockSpec((d_model, d_sec), lambda i, c: (0, c)),
            pl.BlockSpec((d_model, d_sec), lambda i, c: (0, 5)),
            pl.BlockSpec((1, HEAD_DIM), lambda i, c: (0, 0)),
            pl.BlockSpec((1, HEAD_DIM), lambda i, c: (0, 0)),
        ],
        out_specs=[sec_out] * 5,
        out_shape=[out_sds] * 5,
        scratch_shapes=[pltpu.VMEM((tm, d_model), BF16)],
        compiler_params=pltpu.CompilerParams(
            dimension_semantics=("parallel", "arbitrary"),
            vmem_limit_bytes=V7X_VMEM_LIMIT_BYTES),
        name="in_proj",
    )(x, gain, w_in, w_in, q_gain, k_gain)


def _attn_kernel(q_ref, k_ref, v_ref, tri_ref, o_ref, acc_ref, c_ref, *, n_sub, scale):
    qi = pl.program_id(2)
    acc_ref[...] = jnp.zeros_like(acc_ref)
    c_ref[...] = jnp.zeros_like(c_ref)

    row = lax.broadcasted_iota(jnp.int32, (KEY_BLOCK, KEY_BLOCK), 0)
    col = lax.broadcasted_iota(jnp.int32, (KEY_BLOCK, KEY_BLOCK), 1)
    strictly_causal = col < row

    def tile(m, key_block, mask, mask_sums):
        rows = slice(m * KEY_BLOCK, (m + 1) * KEY_BLOCK)
        start = pl.multiple_of(key_block * KEY_BLOCK, KEY_BLOCK)
        z = lax.dot_general(q_ref[0, rows, :], k_ref[0, pl.ds(start, KEY_BLOCK), :],
                            (((1,), (1,)), ((), ())),
                            preferred_element_type=F32) * scale
        sp = jnp.maximum(z, 0.0) + jnp.log(1.0 + jnp.exp(-jnp.abs(z)))
        sp_sum = jnp.where(mask, sp, 0.0) if mask_sums else sp
        hi = sp_sum.astype(BF16)
        lo = (sp_sum - hi.astype(F32)).astype(BF16)
        sums = jnp.dot(jnp.concatenate([hi, lo], axis=1), tri_ref[...],
                       preferred_element_type=F32)
        c = c_ref[rows, :]
        w = jnp.where(mask, jnp.exp(z - sp - sums[:, :KEY_BLOCK] - c), 0.0)
        c_ref[rows, :] = c + sums[:, KEY_BLOCK:]
        acc_ref[rows, :] += jnp.dot(w.astype(BF16), v_ref[0, pl.ds(start, KEY_BLOCK), :],
                                    preferred_element_type=F32)

    first = qi * n_sub
    for m in range(n_sub):
        tile(m, first + m, strictly_causal, True)

    def keep_walking(carry):
        r, exhausted = carry
        return jnp.logical_and(r <= first + n_sub - 1, jnp.logical_not(exhausted))

    def walk(carry):
        r, _ = carry
        for m in range(n_sub):
            kb = first + m - r
            tile(m, jnp.maximum(kb, 0), kb >= 0, False)
        return r + 1, jnp.min(c_ref[...]) > STICK_EXHAUSTED

    lax.while_loop(keep_walking, walk, (jnp.int32(1), jnp.bool_(False)))
    o_ref[0] = acc_ref[...].astype(o_ref.dtype)


def _suffix_sum_matrix():
    j = lax.broadcasted_iota(jnp.int32, (2 * KEY_BLOCK, 2 * KEY_BLOCK), 0) % KEY_BLOCK
    s = lax.broadcasted_iota(jnp.int32, (2 * KEY_BLOCK, 2 * KEY_BLOCK), 1)
    return jnp.where(jnp.logical_or(s >= KEY_BLOCK, j > s), 1.0, 0.0).astype(BF16)


def _attention(q, k, v, *, n_heads, n_sub):
    batch, seq, _ = q.shape
    tq = n_sub * KEY_BLOCK
    q_spec = pl.BlockSpec((1, tq, HEAD_DIM), lambda b, h, i: (b, i, h))
    kv_spec = pl.BlockSpec((1, seq, HEAD_DIM), lambda b, h, i: (b, 0, h))
    return pl.pallas_call(
        functools.partial(_attn_kernel, n_sub=n_sub, scale=1.0 / math.sqrt(HEAD_DIM)),
        grid=(batch, n_heads, seq // tq),
        in_specs=[
            q_spec, kv_spec, kv_spec,
            pl.BlockSpec((2 * KEY_BLOCK, 2 * KEY_BLOCK), lambda b, h, i: (0, 0)),
        ],
        out_specs=q_spec,
        out_shape=jax.ShapeDtypeStruct(q.shape, BF16),
        scratch_shapes=[pltpu.VMEM((tq, HEAD_DIM), F32),
                        pltpu.VMEM((tq, KEY_BLOCK), F32)],
        compiler_params=pltpu.CompilerParams(
            dimension_semantics=("parallel", "parallel", "arbitrary"),
            vmem_limit_bytes=V7X_VMEM_LIMIT_BYTES),
        name="attention",
    )(q, k, v, _suffix_sum_matrix())


def _out_proj_kernel(x_ref, ya_ref, gb_ref, p_ref, halo_ref, cw_ref, cb_ref,
                     ag_ref, cg_ref, w_ref, o_ref, pbuf_ref, *, tiles_per_seq):
    tm = x_ref.shape[0]
    at_seq_start = pl.program_id(0) % tiles_per_seq == 0
    halo = halo_ref[...].astype(F32)
    pbuf_ref[0:CONV_HALO_ROWS, :] = jnp.where(at_seq_start, 0.0, halo)
    pbuf_ref[CONV_HALO_ROWS:, :] = p_ref[...].astype(F32)

    y = cb_ref[...]
    for i in range(CONV_WIDTH):
        off = CONV_HALO_ROWS - (CONV_WIDTH - 1) + i
        y = y + pbuf_ref[off:off + tm, :] * cw_ref[i:i + 1, :]
    y_conv = _rms_norm(gb_ref[...].astype(F32) * y, cg_ref[...]).astype(BF16)
    y_attn = _rms_norm(ya_ref[...].astype(F32), ag_ref[...]).astype(BF16)
    mix = jnp.concatenate([y_attn, y_conv], axis=1)
    o_ref[...] = x_ref[...] + jnp.dot(mix, w_ref[...], preferred_element_type=F32)


def _out_proj(x, ya, gb, p, conv_w, conv_b, attn_gain, conv_gain, w_out, *, tm, seq):
    n_tok, d_model = x.shape
    d_sec = ya.shape[1]
    halo_blocks_per_tile = tm // CONV_HALO_ROWS
    sec = pl.BlockSpec((tm, d_sec), lambda i: (i, 0))
    vec = pl.BlockSpec((1, d_sec), lambda i: (0, 0))
    return pl.pallas_call(
        functools.partial(_out_proj_kernel, tiles_per_seq=seq // tm),
        grid=(n_tok // tm,),
        in_specs=[
            pl.BlockSpec((tm, d_model), lambda i: (i, 0)),
            sec, sec, sec,
            pl.BlockSpec((CONV_HALO_ROWS, d_sec),
                         lambda i: (jnp.maximum(i * halo_blocks_per_tile - 1, 0), 0)),
            pl.BlockSpec((CONV_WIDTH, d_sec), lambda i: (0, 0)),
            vec, vec, vec,
            pl.BlockSpec((d_model, d_model), lambda i: (0, 0)),
        ],
        out_specs=pl.BlockSpec((tm, d_model), lambda i: (i, 0)),
        out_shape=jax.ShapeDtypeStruct((n_tok, d_model), F32),
        scratch_shapes=[pltpu.VMEM((tm + CONV_HALO_ROWS, d_sec), F32)],
        compiler_params=pltpu.CompilerParams(
            dimension_semantics=("parallel",),
            vmem_limit_bytes=V7X_VMEM_LIMIT_BYTES),
        name="out_proj",
    )(x, ya, gb, p, p, conv_w, conv_b, attn_gain, conv_gain, w_out)


def kernel(x, ffn1_norm, ffn1_w_gate, ffn1_w_up, ffn1_w_down, mix_norm, w_in, q_norm, k_norm, conv_w, conv_b, attn_out_norm, conv_out_norm, w_out, ffn2_norm, ffn2_w_gate, ffn2_w_up, ffn2_w_down):
    batch, seq, d_model = x.shape
    depth = w_in.shape[0]
    n_heads = (w_in.shape[2] // 6) // HEAD_DIM
    n_tok = batch * seq
    row = lambda v: v.reshape(1, -1)

    h = x.reshape(n_tok, d_model)
    for l in range(depth):
        h = _ffn(h, row(ffn1_norm[l]), ffn1_w_gate[l].astype(BF16),
                 ffn1_w_up[l].astype(BF16), ffn1_w_down[l].astype(BF16),
                 tm=512, tf=512, n_chunk=512)
        q, k, v, gb, p = _in_proj(h, row(mix_norm[l]), w_in[l].astype(BF16),
                                  row(q_norm[l]), row(k_norm[l]), tm=512)
        to_seq = lambda t: t.reshape(batch, seq, -1)
        ya = _attention(to_seq(q), to_seq(k), to_seq(v), n_heads=n_heads, n_sub=4)
        h = _out_proj(h, ya.reshape(n_tok, -1), gb, p, conv_w[l], row(conv_b[l]),
                      row(attn_out_norm[l]), row(conv_out_norm[l]),
                      w_out[l].astype(BF16), tm=256, seq=seq)
        h = _ffn(h, row(ffn2_norm[l]), ffn2_w_gate[l].astype(BF16),
                 ffn2_w_up[l].astype(BF16), ffn2_w_down[l].astype(BF16),
                 tm=512, tf=512, n_chunk=512)
    return h.reshape(batch, seq, d_model)
```

```python
import functools
import math

import jax
import jax.numpy as jnp
from jax import lax
from jax.experimental import pallas as pl
from jax.experimental.pallas import tpu as pltpu

F32 = jnp.float32
BF16 = jnp.bfloat16

EPS = 1e-6
FFN_RES_WEIGHT = 0.5
HEAD_DIM = 128
CONV_WIDTH = 3
KEY_BLOCK = 128
CONV_HALO_ROWS = 8

STICK_EXHAUSTED = 110.0

V7X_VMEM_LIMIT_BYTES = 56 * 1024 * 1024


def _rms_norm(x, gain):
    ms = jnp.mean(x * x, axis=-1, keepdims=True)
    return x * lax.rsqrt(ms + EPS) * gain


def _ffn_kernel(x_ref, g_ref, wg_ref, wu_ref, wd_ref, o_ref, xn_ref, *, n_chunk):
    j = pl.program_id(1)

    @pl.when(j == 0)
    def _():
        x = x_ref[...]
        xn_ref[...] = _rms_norm(x, g_ref[...]).astype(BF16)
        o_ref[...] = x

    xn = xn_ref[...]
    hg = jnp.dot(xn, wg_ref[...], preferred_element_type=F32)
    hu = jnp.dot(xn, wu_ref[...], preferred_element_type=F32)
    act = (hg * jax.nn.sigmoid(hg) * hu * FFN_RES_WEIGHT).astype(BF16)
    d_model = o_ref.shape[1]
    for n in range(0, d_model, n_chunk):
        o_ref[:, n:n + n_chunk] += jnp.dot(
            act, wd_ref[:, n:n + n_chunk], preferred_element_type=F32)


def _ffn(x, gain, wg, wu, wd, *, tm, tf, n_chunk):
    n_tok, d_model = x.shape
    d_ff = wg.shape[1]
    grid = (n_tok // tm, d_ff // tf)
    return pl.pallas_call(
        functools.partial(_ffn_kernel, n_chunk=n_chunk),
        grid=grid,
        in_specs=[
            pl.BlockSpec((tm, d_model), lambda i, j: (i, 0)),
            pl.BlockSpec((1, d_model), lambda i, j: (0, 0)),
            pl.BlockSpec((d_model, tf), lambda i, j: (0, j)),
            pl.BlockSpec((d_model, tf), lambda i, j: (0, j)),
            pl.BlockSpec((tf, d_model), lambda i, j: (j, 0)),
        ],
        out_specs=pl.BlockSpec((tm, d_model), lambda i, j: (i, 0)),
        out_shape=jax.ShapeDtypeStruct((n_tok, d_model), F32),
        scratch_shapes=[pltpu.VMEM((tm, d_model), BF16)],
        compiler_params=pltpu.CompilerParams(
            dimension_semantics=("parallel", "arbitrary"),
            vmem_limit_bytes=V7X_VMEM_LIMIT_BYTES),
        name="ffn",
    )(x, gain, wg, wu, wd)


def _in_proj_kernel(x_ref, g_ref, w_ref, qg_ref, kg_ref,
                    q_ref, k_ref, v_ref, gb_ref, p_ref):
    xn = _rms_norm(x_ref[...], g_ref[...]).astype(BF16)
    d_sec = q_ref.shape[1]

    def section(s):
        return jnp.dot(xn, w_ref[:, s * d_sec:(s + 1) * d_sec],
                       preferred_element_type=F32)

    def head_norm(a, gain, out_ref):
        for h in range(d_sec // HEAD_DIM):
            cols = slice(h * HEAD_DIM, (h + 1) * HEAD_DIM)
            out_ref[:, cols] = _rms_norm(a[:, cols], gain).astype(BF16)

    head_norm(section(0), qg_ref[...], q_ref)
    head_norm(section(1), kg_ref[...], k_ref)
    v_ref[...] = section(2).astype(BF16)
    gb_ref[...] = section(3).astype(BF16)
    p_ref[...] = (section(4) * section(5)).astype(BF16)


def _in_proj(x, gain, w_in, q_gain, k_gain, *, tm):
    n_tok, d_model = x.shape
    d_sec = w_in.shape[1] // 6
    sec_out = pl.BlockSpec((tm, d_sec), lambda i: (i, 0))
    out_sds = jax.ShapeDtypeStruct((n_tok, d_sec), BF16)
    return pl.pallas_call(
        _in_proj_kernel,
        grid=(n_tok // tm,),
        in_specs=[
            pl.BlockSpec((tm, d_model), lambda i: (i, 0)),
            pl.BlockSpec((1, d_model), lambda i: (0, 0)),
            pl.BlockSpec(w_in.shape, lambda i: (0, 0), pipeline_mode=pl.Buffered(1)),
            pl.BlockSpec((1, HEAD_DIM), lambda i: (0, 0)),
            pl.BlockSpec((1, HEAD_DIM), lambda i: (0, 0)),
        ],
        out_specs=[sec_out] * 5,
        out_shape=[out_sds] * 5,
        compiler_params=pltpu.CompilerParams(
            dimension_semantics=("parallel",),
            vmem_limit_bytes=V7X_VMEM_LIMIT_BYTES),
        name="in_proj",
    )(x, gain, w_in, q_gain, k_gain)


def _attn_kernel(q_ref, k_ref, v_ref, tri_ref, o_ref, acc_ref, c_ref,
                 *, n_sub, scale, peeled_steps):
    qi = pl.program_id(2)
    first = qi * n_sub
    acc_ref[...] = jnp.zeros_like(acc_ref)
    c_ref[...] = jnp.zeros_like(c_ref)

    row = lax.broadcasted_iota(jnp.int32, (KEY_BLOCK, KEY_BLOCK), 0)
    col = lax.broadcasted_iota(jnp.int32, (KEY_BLOCK, KEY_BLOCK), 1)
    strictly_causal = col < row
    sub_rows = [slice(m * KEY_BLOCK, (m + 1) * KEY_BLOCK) for m in range(n_sub)]

    def step(r):
        on_diagonal = isinstance(r, int) and r == 0
        starts, live, log_beta, lhs = [], [], [], []
        for m in range(n_sub):
            kb = first + m - r
            live.append(strictly_causal if on_diagonal else kb >= 0)
            start = pl.multiple_of(jnp.maximum(kb, 0) * KEY_BLOCK, KEY_BLOCK)
            starts.append(start)
            z = lax.dot_general(q_ref[0, sub_rows[m], :], k_ref[0, pl.ds(start, KEY_BLOCK), :],
                                (((1,), (1,)), ((), ())),
                                preferred_element_type=F32) * scale
            sp = jnp.maximum(z, 0.0) + jnp.log(1.0 + jnp.exp(-jnp.abs(z)))
            log_beta.append(z - sp)
            sp_sum = jnp.where(strictly_causal, sp, 0.0) if on_diagonal else sp
            hi = sp_sum.astype(BF16)
            lo = (sp_sum - hi.astype(F32)).astype(BF16)
            lhs.append(jnp.concatenate([hi, lo], axis=1))
        sums = jnp.dot(jnp.concatenate(lhs, axis=0), tri_ref[...],
                       preferred_element_type=F32)
        c = c_ref[...]
        c_new = c + sums[:, KEY_BLOCK:]
        c_ref[...] = c_new
        tail = sums[:, :KEY_BLOCK] + c
        mass_left = None
        for m in range(n_sub):
            w = jnp.where(live[m], jnp.exp(log_beta[m] - tail[sub_rows[m], :]), 0.0)
            acc_ref[sub_rows[m], :] += jnp.dot(
                w.astype(BF16), v_ref[0, pl.ds(starts[m], KEY_BLOCK), :],
                preferred_element_type=F32)
            if not on_diagonal:
                sub_min = jnp.min(c_new[sub_rows[m], :].reshape(-1, 8, KEY_BLOCK), axis=0)
                sub_min = jnp.where(live[m], sub_min, jnp.inf)
                mass_left = sub_min if mass_left is None else jnp.minimum(mass_left, sub_min)
        return None if on_diagonal else jnp.min(mass_left)

    step(0)
    min_mass = jnp.float32(0.0)
    for r in range(1, peeled_steps + 1):
        min_mass = step(r)

    def keep_walking(carry):
        r, min_mass = carry
        return jnp.logical_and(r <= first + n_sub - 1, min_mass <= STICK_EXHAUSTED)

    def walk(carry):
        r, _ = carry
        return r + 1, step(r)

    lax.while_loop(keep_walking, walk, (jnp.int32(peeled_steps + 1), min_mass))
    o_ref[0] = acc_ref[...].astype(o_ref.dtype)


def _suffix_sum_matrix():
    j = lax.broadcasted_iota(jnp.int32, (2 * KEY_BLOCK, 2 * KEY_BLOCK), 0) % KEY_BLOCK
    s = lax.broadcasted_iota(jnp.int32, (2 * KEY_BLOCK, 2 * KEY_BLOCK), 1)
    return jnp.where(jnp.logical_or(s >= KEY_BLOCK, j > s), 1.0, 0.0).astype(BF16)


def _attention(q, k, v, *, n_heads, n_sub, peeled_steps):
    batch, seq, _ = q.shape
    tq = n_sub * KEY_BLOCK
    q_spec = pl.BlockSpec((1, tq, HEAD_DIM), lambda b, h, i: (b, i, h))
    kv_spec = pl.BlockSpec((1, seq, HEAD_DIM), lambda b, h, i: (b, 0, h))
    return pl.pallas_call(
        functools.partial(_attn_kernel, n_sub=n_sub, scale=1.0 / math.sqrt(HEAD_DIM),
                          peeled_steps=peeled_steps),
        grid=(batch, n_heads, seq // tq),
        in_specs=[
            q_spec, kv_spec, kv_spec,
            pl.BlockSpec((2 * KEY_BLOCK, 2 * KEY_BLOCK), lambda b, h, i: (0, 0)),
        ],
        out_specs=q_spec,
        out_shape=jax.ShapeDtypeStruct(q.shape, BF16),
        scratch_shapes=[pltpu.VMEM((tq, HEAD_DIM), F32),
                        pltpu.VMEM((tq, KEY_BLOCK), F32)],
        compiler_params=pltpu.CompilerParams(
            dimension_semantics=("parallel", "parallel", "arbitrary"),
            vmem_limit_bytes=V7X_VMEM_LIMIT_BYTES),
        name="attention",
    )(q, k, v, _suffix_sum_matrix())


def _out_proj_kernel(x_ref, ya_ref, gb_ref, p_ref, halo_ref, cw_ref, cb_ref,
                     ag_ref, cg_ref, w_ref, o_ref, pbuf_ref, *, tiles_per_seq):
    tm = x_ref.shape[0]
    at_seq_start = pl.program_id(0) % tiles_per_seq == 0
    halo = halo_ref[...].astype(F32)
    pbuf_ref[0:CONV_HALO_ROWS, :] = jnp.where(at_seq_start, 0.0, halo)
    pbuf_ref[CONV_HALO_ROWS:, :] = p_ref[...].astype(F32)

    y = cb_ref[...]
    for i in range(CONV_WIDTH):
        off = CONV_HALO_ROWS - (CONV_WIDTH - 1) + i
        y = y + pbuf_ref[off:off + tm, :] * cw_ref[i:i + 1, :]
    y_conv = _rms_norm(gb_ref[...].astype(F32) * y, cg_ref[...]).astype(BF16)
    y_attn = _rms_norm(ya_ref[...].astype(F32), ag_ref[...]).astype(BF16)
    mix = jnp.concatenate([y_attn, y_conv], axis=1)
    o_ref[...] = x_ref[...] + jnp.dot(mix, w_ref[...], preferred_element_type=F32)


def _out_proj(x, ya, gb, p, conv_w, conv_b, attn_gain, conv_gain, w_out, *, tm, seq):
    n_tok, d_model = x.shape
    d_sec = ya.shape[1]
    halo_blocks_per_tile = tm // CONV_HALO_ROWS
    sec = pl.BlockSpec((tm, d_sec), lambda i: (i, 0))
    vec = pl.BlockSpec((1, d_sec), lambda i: (0, 0))
    return pl.pallas_call(
        functools.partial(_out_proj_kernel, tiles_per_seq=seq // tm),
        grid=(n_tok // tm,),
        in_specs=[
            pl.BlockSpec((tm, d_model), lambda i: (i, 0)),
            sec, sec, sec,
            pl.BlockSpec((CONV_HALO_ROWS, d_sec),
                         lambda i: (jnp.maximum(i * halo_blocks_per_tile - 1, 0), 0)),
            pl.BlockSpec((CONV_WIDTH, d_sec), lambda i: (0, 0)),
            vec, vec, vec,
            pl.BlockSpec((d_model, d_model), lambda i: (0, 0)),
        ],
        out_specs=pl.BlockSpec((tm, d_model), lambda i: (i, 0)),
        out_shape=jax.ShapeDtypeStruct((n_tok, d_model), F32),
        scratch_shapes=[pltpu.VMEM((tm + CONV_HALO_ROWS, d_sec), F32)],
        compiler_params=pltpu.CompilerParams(
            dimension_semantics=("parallel",),
            vmem_limit_bytes=V7X_VMEM_LIMIT_BYTES),
        name="out_proj",
    )(x, ya, gb, p, p, conv_w, conv_b, attn_gain, conv_gain, w_out)


def kernel(x, ffn1_norm, ffn1_w_gate, ffn1_w_up, ffn1_w_down, mix_norm, w_in, q_norm, k_norm, conv_w, conv_b, attn_out_norm, conv_out_norm, w_out, ffn2_norm, ffn2_w_gate, ffn2_w_up, ffn2_w_down):
    batch, seq, d_model = x.shape
    depth = w_in.shape[0]
    n_heads = (w_in.shape[2] // 6) // HEAD_DIM
    n_tok = batch * seq
    row = lambda v: v.reshape(1, -1)

    h = x.reshape(n_tok, d_model)
    for l in range(depth):
        h = _ffn(h, row(ffn1_norm[l]), ffn1_w_gate[l].astype(BF16),
                 ffn1_w_up[l].astype(BF16), ffn1_w_down[l].astype(BF16),
                 tm=512, tf=512, n_chunk=512)
        q, k, v, gb, p = _in_proj(h, row(mix_norm[l]), w_in[l].astype(BF16),
                                  row(q_norm[l]), row(k_norm[l]), tm=512)
        to_seq = lambda t: t.reshape(batch, seq, -1)
        ya = _attention(to_seq(q), to_seq(k), to_seq(v), n_heads=n_heads, n_sub=8,
                        peeled_steps=2)
        h = _out_proj(h, ya.reshape(n_tok, -1), gb, p, conv_w[l], row(conv_b[l]),
                      row(attn_out_norm[l]), row(conv_out_norm[l]),
                      w_out[l].astype(BF16), tm=256, seq=seq)
        h = _ffn(h, row(ffn2_norm[l]), ffn2_w_gate[l].astype(BF16),
                 ffn2_w_up[l].astype(BF16), ffn2_w_down[l].astype(BF16),
                 tm=512, tf=512, n_chunk=512)
    return h.reshape(batch, seq, d_model)
```

```python
import functools
import math

import jax
import jax.numpy as jnp
from jax import lax
from jax.experimental import pallas as pl
from jax.experimental.pallas import tpu as pltpu

F32 = jnp.float32
BF16 = jnp.bfloat16

EPS = 1e-6
FFN_RES_WEIGHT = 0.5
HEAD_DIM = 128
CONV_WIDTH = 3
KEY_BLOCK = 128
CONV_HALO_ROWS = 8

STICK_EXHAUSTED = 110.0

V7X_VMEM_LIMIT_BYTES = 56 * 1024 * 1024


def _rms_norm(x, gain):
    ms = jnp.mean(x * x, axis=-1, keepdims=True)
    return x * lax.rsqrt(ms + EPS) * gain


def _ffn_kernel(x_hbm, g_ref, wg_ref, wu_ref, wd_ref, o_ref, x_buf, xn_ref, x_sem,
                *, out_chunk):
    i = pl.program_id(0)
    j = pl.program_id(1)
    tm, d_model = o_ref.shape

    def x_copy(tile):
        rows = pl.ds(pl.multiple_of(tile * tm, tm), tm)
        return pltpu.make_async_copy(x_hbm.at[rows, :], x_buf, x_sem)

    @pl.when(j == 0)
    def _():
        @pl.when(i == 0)
        def _():
            x_copy(0).start()

        x_copy(i).wait()
        x = x_buf[...]
        xn_ref[...] = _rms_norm(x, g_ref[...]).astype(BF16)
        o_ref[...] = x

        @pl.when(i + 1 < pl.num_programs(0))
        def _():
            x_copy(i + 1).start()

    xn = xn_ref[...]
    hg = jnp.dot(xn, wg_ref[...], preferred_element_type=F32)
    hu = jnp.dot(xn, wu_ref[...], preferred_element_type=F32)
    act = (hg * jax.nn.sigmoid(hg) * hu * FFN_RES_WEIGHT).astype(BF16)
    for n in range(0, d_model, out_chunk):
        o_ref[:, n:n + out_chunk] += jnp.dot(
            act, wd_ref[:, n:n + out_chunk], preferred_element_type=F32)


def _ffn(x, gain, wg, wu, wd, *, tm, tf, out_chunk):
    n_tok, d_model = x.shape
    d_ff = wg.shape[1]
    return pl.pallas_call(
        functools.partial(_ffn_kernel, out_chunk=out_chunk),
        grid=(n_tok // tm, d_ff // tf),
        in_specs=[
            pl.BlockSpec(memory_space=pl.ANY),
            pl.BlockSpec((1, d_model), lambda i, j: (0, 0)),
            pl.BlockSpec((d_model, tf), lambda i, j: (0, j)),
            pl.BlockSpec((d_model, tf), lambda i, j: (0, j)),
            pl.BlockSpec((tf, d_model), lambda i, j: (j, 0)),
        ],
        out_specs=pl.BlockSpec((tm, d_model), lambda i, j: (i, 0)),
        out_shape=jax.ShapeDtypeStruct((n_tok, d_model), F32),
        scratch_shapes=[pltpu.VMEM((tm, d_model), F32),
                        pltpu.VMEM((tm, d_model), BF16),
                        pltpu.SemaphoreType.DMA(())],
        compiler_params=pltpu.CompilerParams(
            dimension_semantics=("arbitrary", "arbitrary"),
            vmem_limit_bytes=V7X_VMEM_LIMIT_BYTES),
        name="ffn",
    )(x, gain, wg, wu, wd)


def _in_proj_kernel(x_ref, g_ref, w_ref, qg_ref, kg_ref, cw_ref, cb_ref, cg_ref,
                    q_ref, k_ref, v_ref, yc_ref, cbuf_ref, *, tiles_per_seq):
    tm = x_ref.shape[0]
    d_sec = q_ref.shape[1]
    at_seq_start = pl.program_id(0) % tiles_per_seq == 0

    @pl.when(at_seq_start)
    def _():
        cbuf_ref[0:CONV_HALO_ROWS, :] = jnp.zeros((CONV_HALO_ROWS, d_sec), F32)

    @pl.when(jnp.logical_not(at_seq_start))
    def _():
        cbuf_ref[0:CONV_HALO_ROWS, :] = cbuf_ref[tm:tm + CONV_HALO_ROWS, :]

    xn = _rms_norm(x_ref[...], g_ref[...]).astype(BF16)

    def section(s):
        return jnp.dot(xn, w_ref[:, s * d_sec:(s + 1) * d_sec],
                       preferred_element_type=F32)

    def head_norm(a, gain, out_ref):
        for h in range(d_sec // HEAD_DIM):
            cols = slice(h * HEAD_DIM, (h + 1) * HEAD_DIM)
            out_ref[:, cols] = _rms_norm(a[:, cols], gain).astype(BF16)

    gate = section(3)
    cbuf_ref[CONV_HALO_ROWS:, :] = section(4) * section(5)
    y = cb_ref[...]
    for i in range(CONV_WIDTH):
        off = CONV_HALO_ROWS - (CONV_WIDTH - 1) + i
        y = y + cbuf_ref[off:off + tm, :] * cw_ref[i:i + 1, :]
    yc_ref[...] = _rms_norm(gate * y, cg_ref[...]).astype(BF16)

    head_norm(section(0), qg_ref[...], q_ref)
    head_norm(section(1), kg_ref[...], k_ref)
    v_ref[...] = section(2).astype(BF16)


def _in_proj(x, gain, w_in, q_gain, k_gain, conv_w, conv_b, conv_gain, *, tm, seq):
    n_tok, d_model = x.shape
    d_sec = w_in.shape[1] // 6
    sec_out = pl.BlockSpec((tm, d_sec), lambda i: (i, 0))
    out_sds = jax.ShapeDtypeStruct((n_tok, d_sec), BF16)
    head_vec = pl.BlockSpec((1, HEAD_DIM), lambda i: (0, 0))
    sec_vec = pl.BlockSpec((1, d_sec), lambda i: (0, 0))
    return pl.pallas_call(
        functools.partial(_in_proj_kernel, tiles_per_seq=seq // tm),
        grid=(n_tok // tm,),
        in_specs=[
            pl.BlockSpec((tm, d_model), lambda i: (i, 0)),
            pl.BlockSpec((1, d_model), lambda i: (0, 0)),
            pl.BlockSpec(w_in.shape, lambda i: (0, 0), pipeline_mode=pl.Buffered(1)),
            head_vec, head_vec,
            pl.BlockSpec((CONV_WIDTH, d_sec), lambda i: (0, 0)),
            sec_vec, sec_vec,
        ],
        out_specs=[sec_out] * 4,
        out_shape=[out_sds] * 4,
        scratch_shapes=[pltpu.VMEM((tm + CONV_HALO_ROWS, d_sec), F32)],
        compiler_params=pltpu.CompilerParams(
            dimension_semantics=("arbitrary",),
            vmem_limit_bytes=V7X_VMEM_LIMIT_BYTES),
        name="in_proj",
    )(x, gain, w_in, q_gain, k_gain, conv_w, conv_b, conv_gain)


def _attn_kernel(q_ref, k_ref, v_ref, tri_ref, o_ref, acc_ref, c_ref,
                 *, n_sub, scale, peeled_steps):
    qi = pl.program_id(2)
    first = qi * n_sub
    acc_ref[...] = jnp.zeros_like(acc_ref)
    c_ref[...] = jnp.zeros_like(c_ref)

    row = lax.broadcasted_iota(jnp.int32, (KEY_BLOCK, KEY_BLOCK), 0)
    col = lax.broadcasted_iota(jnp.int32, (KEY_BLOCK, KEY_BLOCK), 1)
    strictly_causal = col < row
    sub_rows = [slice(m * KEY_BLOCK, (m + 1) * KEY_BLOCK) for m in range(n_sub)]

    def step(r):
        on_diagonal = isinstance(r, int) and r == 0
        starts, live, log_beta, lhs = [], [], [], []
        for m in range(n_sub):
            kb = first + m - r
            live.append(strictly_causal if on_diagonal else kb >= 0)
            start = pl.multiple_of(jnp.maximum(kb, 0) * KEY_BLOCK, KEY_BLOCK)
            starts.append(start)
            z = lax.dot_general(q_ref[0, sub_rows[m], :], k_ref[0, pl.ds(start, KEY_BLOCK), :],
                                (((1,), (1,)), ((), ())),
                                preferred_element_type=F32) * scale
            sp = jnp.maximum(z, 0.0) + jnp.log(1.0 + jnp.exp(-jnp.abs(z)))
            log_beta.append(z - sp)
            sp_sum = jnp.where(strictly_causal, sp, 0.0) if on_diagonal else sp
            hi = sp_sum.astype(BF16)
            lo = (sp_sum - hi.astype(F32)).astype(BF16)
            lhs.append(jnp.concatenate([hi, lo], axis=1))
        sums = jnp.dot(jnp.concatenate(lhs, axis=0), tri_ref[...],
                       preferred_element_type=F32)
        c = c_ref[...]
        c_new = c + sums[:, KEY_BLOCK:]
        c_ref[...] = c_new
        tail = sums[:, :KEY_BLOCK] + c
        mass_left = None
        for m in range(n_sub):
            w = jnp.where(live[m], jnp.exp(log_beta[m] - tail[sub_rows[m], :]), 0.0)
            acc_ref[sub_rows[m], :] += jnp.dot(
                w.astype(BF16), v_ref[0, pl.ds(starts[m], KEY_BLOCK), :],
                preferred_element_type=F32)
            if not on_diagonal:
                sub_min = jnp.min(c_new[sub_rows[m], :].reshape(-1, 8, KEY_BLOCK), axis=0)
                sub_min = jnp.where(live[m], sub_min, jnp.inf)
                mass_left = sub_min if mass_left is None else jnp.minimum(mass_left, sub_min)
        return None if on_diagonal else jnp.min(mass_left)

    step(0)
    min_mass = jnp.float32(0.0)
    for r in range(1, peeled_steps + 1):
        min_mass = step(r)

    def keep_walking(carry):
        r, min_mass = carry
        return jnp.logical_and(r <= first + n_sub - 1, min_mass <= STICK_EXHAUSTED)

    def walk(carry):
        r, _ = carry
        return r + 1, step(r)

    lax.while_loop(keep_walking, walk, (jnp.int32(peeled_steps + 1), min_mass))
    o_ref[0] = acc_ref[...].astype(o_ref.dtype)


def _suffix_sum_matrix():
    j = lax.broadcasted_iota(jnp.int32, (2 * KEY_BLOCK, 2 * KEY_BLOCK), 0) % KEY_BLOCK
    s = lax.broadcasted_iota(jnp.int32, (2 * KEY_BLOCK, 2 * KEY_BLOCK), 1)
    return jnp.where(jnp.logical_or(s >= KEY_BLOCK, j > s), 1.0, 0.0).astype(BF16)


def _attention(q, k, v, *, n_heads, n_sub, peeled_steps):
    batch, seq, _ = q.shape
    tq = n_sub * KEY_BLOCK
    q_spec = pl.BlockSpec((1, tq, HEAD_DIM), lambda b, h, i: (b, i, h))
    kv_spec = pl.BlockSpec((1, seq, HEAD_DIM), lambda b, h, i: (b, 0, h))
    return pl.pallas_call(
        functools.partial(_attn_kernel, n_sub=n_sub, scale=1.0 / math.sqrt(HEAD_DIM),
                          peeled_steps=peeled_steps),
        grid=(batch, n_heads, seq // tq),
        in_specs=[
            q_spec, kv_spec, kv_spec,
            pl.BlockSpec((2 * KEY_BLOCK, 2 * KEY_BLOCK), lambda b, h, i: (0, 0)),
        ],
        out_specs=q_spec,
        out_shape=jax.ShapeDtypeStruct(q.shape, BF16),
        scratch_shapes=[pltpu.VMEM((tq, HEAD_DIM), F32),
                        pltpu.VMEM((tq, KEY_BLOCK), F32)],
        compiler_params=pltpu.CompilerParams(
            dimension_semantics=("parallel", "parallel", "arbitrary"),
            vmem_limit_bytes=V7X_VMEM_LIMIT_BYTES),
        name="attention",
    )(q, k, v, _suffix_sum_matrix())


def _out_proj_kernel(x_ref, ya_ref, yc_ref, ag_ref, w_ref, o_ref):
    y_attn = _rms_norm(ya_ref[...].astype(F32), ag_ref[...]).astype(BF16)
    mix = jnp.concatenate([y_attn, yc_ref[...]], axis=1)
    o_ref[...] = x_ref[...] + jnp.dot(mix, w_ref[...], preferred_element_type=F32)


def _out_proj(x, ya, yc, attn_gain, w_out, *, tm):
    n_tok, d_model = x.shape
    d_sec = ya.shape[1]
    sec = pl.BlockSpec((tm, d_sec), lambda i: (i, 0))
    return pl.pallas_call(
        _out_proj_kernel,
        grid=(n_tok // tm,),
        in_specs=[
            pl.BlockSpec((tm, d_model), lambda i: (i, 0)),
            sec, sec,
            pl.BlockSpec((1, d_sec), lambda i: (0, 0)),
            pl.BlockSpec((d_model, d_model), lambda i: (0, 0)),
        ],
        out_specs=pl.BlockSpec((tm, d_model), lambda i: (i, 0)),
        out_shape=jax.ShapeDtypeStruct((n_tok, d_model), F32),
        compiler_params=pltpu.CompilerParams(
            dimension_semantics=("parallel",),
            vmem_limit_bytes=V7X_VMEM_LIMIT_BYTES),
        name="out_proj",
    )(x, ya, yc, attn_gain, w_out)


def kernel(x, ffn1_norm, ffn1_w_gate, ffn1_w_up, ffn1_w_down, mix_norm, w_in, q_norm, k_norm, conv_w, conv_b, attn_out_norm, conv_out_norm, w_out, ffn2_norm, ffn2_w_gate, ffn2_w_up, ffn2_w_down):
    batch, seq, d_model = x.shape
    depth = w_in.shape[0]
    n_heads = (w_in.shape[2] // 6) // HEAD_DIM
    n_tok = batch * seq
    row = lambda v: v.reshape(1, -1)

    h = x.reshape(n_tok, d_model)
    for l in range(depth):
        h = _ffn(h, row(ffn1_norm[l]), ffn1_w_gate[l].astype(BF16),
                 ffn1_w_up[l].astype(BF16), ffn1_w_down[l].astype(BF16),
                 tm=1024, tf=512, out_chunk=512)
        q, k, v, yc = _in_proj(h, row(mix_norm[l]), w_in[l].astype(BF16),
                               row(q_norm[l]), row(k_norm[l]), conv_w[l],
                               row(conv_b[l]), row(conv_out_norm[l]), tm=512, seq=seq)
        to_seq = lambda t: t.reshape(batch, seq, -1)
        ya = _attention(to_seq(q), to_seq(k), to_seq(v), n_heads=n_heads, n_sub=8,
                        peeled_steps=2)
        h = _out_proj(h, ya.reshape(n_tok, -1), yc, row(attn_out_norm[l]),
                      w_out[l].astype(BF16), tm=512)
        h = _ffn(h, row(ffn2_norm[l]), ffn2_w_gate[l].astype(BF16),
                 ffn2_w_up[l].astype(BF16), ffn2_w_down[l].astype(BF16),
                 tm=1024, tf=512, out_chunk=512)
    return h.reshape(batch, seq, d_model)
```

```python
import functools
import math

import jax
import jax.numpy as jnp
from jax import lax
from jax.experimental import pallas as pl
from jax.experimental.pallas import tpu as pltpu

F32 = jnp.float32
BF16 = jnp.bfloat16

EPS = 1e-6
LOG2_E = math.log2(math.e)
FFN_RES_WEIGHT = 0.5
HEAD_DIM = 128
CONV_WIDTH = 3
KEY_BLOCK = 128
CONV_HALO_ROWS = 8

STICK_EXHAUSTED = 110.0

V7X_VMEM_LIMIT_BYTES = 56 * 1024 * 1024


def _rms_norm(x, gain):
    ms = jnp.mean(x * x, axis=-1, keepdims=True)
    return x * lax.rsqrt(ms + EPS) * gain


def _ffn_kernel(x_hbm, g_ref, wg_ref, wu_ref, wd_ref, o_ref, x_buf, xn_ref, x_sem,
                *, out_chunk):
    i = pl.program_id(0)
    j = pl.program_id(1)
    tm, d_model = o_ref.shape

    def x_copy(tile):
        rows = pl.ds(pl.multiple_of(tile * tm, tm), tm)
        return pltpu.make_async_copy(x_hbm.at[rows, :], x_buf, x_sem)

    @pl.when(j == 0)
    def _():
        @pl.when(i == 0)
        def _():
            x_copy(0).start()

        x_copy(i).wait()
        x = x_buf[...]
        xn_ref[...] = _rms_norm(x, g_ref[...]).astype(BF16)
        o_ref[...] = x

        @pl.when(i + 1 < pl.num_programs(0))
        def _():
            x_copy(i + 1).start()

    xn = xn_ref[...]
    hg = jnp.dot(xn, wg_ref[...], preferred_element_type=F32)
    hu = jnp.dot(xn, wu_ref[...], preferred_element_type=F32)
    act = (hg * jax.nn.sigmoid(hg) * hu * FFN_RES_WEIGHT).astype(BF16)
    for n in range(0, d_model, out_chunk):
        o_ref[:, n:n + out_chunk] += jnp.dot(
            act, wd_ref[:, n:n + out_chunk], preferred_element_type=F32)


def _ffn(x, gain, wg, wu, wd, *, tm, tf, out_chunk):
    n_tok, d_model = x.shape
    d_ff = wg.shape[1]
    return pl.pallas_call(
        functools.partial(_ffn_kernel, out_chunk=out_chunk),
        grid=(n_tok // tm, d_ff // tf),
        in_specs=[
            pl.BlockSpec(memory_space=pl.ANY),
            pl.BlockSpec((1, d_model), lambda i, j: (0, 0)),
            pl.BlockSpec((d_model, tf), lambda i, j: (0, j)),
            pl.BlockSpec((d_model, tf), lambda i, j: (0, j)),
            pl.BlockSpec((tf, d_model), lambda i, j: (j, 0)),
        ],
        out_specs=pl.BlockSpec((tm, d_model), lambda i, j: (i, 0)),
        out_shape=jax.ShapeDtypeStruct((n_tok, d_model), F32),
        scratch_shapes=[pltpu.VMEM((tm, d_model), F32),
                        pltpu.VMEM((tm, d_model), BF16),
                        pltpu.SemaphoreType.DMA(())],
        compiler_params=pltpu.CompilerParams(
            dimension_semantics=("arbitrary", "arbitrary"),
            vmem_limit_bytes=V7X_VMEM_LIMIT_BYTES),
        name="ffn",
    )(x, gain, wg, wu, wd)


def _in_proj_kernel(x_ref, g_ref, w_ref, qg_ref, kg_ref, cw_ref, cb_ref, cg_ref,
                    q_ref, k_ref, v_ref, yc_ref, cbuf_ref, *, tiles_per_seq):
    tm = x_ref.shape[0]
    d_sec = q_ref.shape[1]
    at_seq_start = pl.program_id(0) % tiles_per_seq == 0

    @pl.when(at_seq_start)
    def _():
        cbuf_ref[0:CONV_HALO_ROWS, :] = jnp.zeros((CONV_HALO_ROWS, d_sec), F32)

    @pl.when(jnp.logical_not(at_seq_start))
    def _():
        cbuf_ref[0:CONV_HALO_ROWS, :] = cbuf_ref[tm:tm + CONV_HALO_ROWS, :]

    xn = _rms_norm(x_ref[...], g_ref[...]).astype(BF16)

    def section(s):
        return jnp.dot(xn, w_ref[:, s * d_sec:(s + 1) * d_sec],
                       preferred_element_type=F32)

    def head_norm(a, gain, out_ref):
        for h in range(d_sec // HEAD_DIM):
            cols = slice(h * HEAD_DIM, (h + 1) * HEAD_DIM)
            out_ref[:, cols] = _rms_norm(a[:, cols], gain).astype(BF16)

    gate = section(3)
    cbuf_ref[CONV_HALO_ROWS:, :] = section(4) * section(5)
    y = cb_ref[...]
    for i in range(CONV_WIDTH):
        off = CONV_HALO_ROWS - (CONV_WIDTH - 1) + i
        y = y + cbuf_ref[off:off + tm, :] * cw_ref[i:i + 1, :]
    yc_ref[...] = _rms_norm(gate * y, cg_ref[...]).astype(BF16)

    head_norm(section(0), qg_ref[...], q_ref)
    head_norm(section(1), kg_ref[...], k_ref)
    v_ref[...] = section(2).astype(BF16)


def _in_proj(x, gain, w_in, q_gain, k_gain, conv_w, conv_b, conv_gain, *, tm, seq):
    n_tok, d_model = x.shape
    d_sec = w_in.shape[1] // 6
    sec_out = pl.BlockSpec((tm, d_sec), lambda i: (i, 0))
    out_sds = jax.ShapeDtypeStruct((n_tok, d_sec), BF16)
    head_vec = pl.BlockSpec((1, HEAD_DIM), lambda i: (0, 0))
    sec_vec = pl.BlockSpec((1, d_sec), lambda i: (0, 0))
    return pl.pallas_call(
        functools.partial(_in_proj_kernel, tiles_per_seq=seq // tm),
        grid=(n_tok // tm,),
        in_specs=[
            pl.BlockSpec((tm, d_model), lambda i: (i, 0)),
            pl.BlockSpec((1, d_model), lambda i: (0, 0)),
            pl.BlockSpec(w_in.shape, lambda i: (0, 0), pipeline_mode=pl.Buffered(1)),
            head_vec, head_vec,
            pl.BlockSpec((CONV_WIDTH, d_sec), lambda i: (0, 0)),
            sec_vec, sec_vec,
        ],
        out_specs=[sec_out] * 4,
        out_shape=[out_sds] * 4,
        scratch_shapes=[pltpu.VMEM((tm + CONV_HALO_ROWS, d_sec), F32)],
        compiler_params=pltpu.CompilerParams(
            dimension_semantics=("arbitrary",),
            vmem_limit_bytes=V7X_VMEM_LIMIT_BYTES),
        name="in_proj",
    )(x, gain, w_in, q_gain, k_gain, conv_w, conv_b, conv_gain)


def _attn_kernel(q_ref, k_ref, v_ref, tri_ref, o_ref, acc_ref, c_ref,
                 *, n_sub, peeled_steps):
    qi = pl.program_id(2)
    first = qi * n_sub
    acc_ref[...] = jnp.zeros_like(acc_ref)

    row = lax.broadcasted_iota(jnp.int32, (KEY_BLOCK, KEY_BLOCK), 0)
    col = lax.broadcasted_iota(jnp.int32, (KEY_BLOCK, KEY_BLOCK), 1)
    strictly_causal = col < row
    sub_rows = [slice(m * KEY_BLOCK, (m + 1) * KEY_BLOCK) for m in range(n_sub)]

    def scores(r):
        on_diagonal = isinstance(r, int) and r == 0
        starts, live, log_beta, lhs = [], [], [], []
        for m in range(n_sub):
            kb = first + m - r
            live.append(strictly_causal if on_diagonal else kb >= 0)
            start = pl.multiple_of(jnp.maximum(kb, 0) * KEY_BLOCK, KEY_BLOCK)
            starts.append(start)
            z = lax.dot_general(q_ref[0, sub_rows[m], :], k_ref[0, pl.ds(start, KEY_BLOCK), :],
                                (((1,), (1,)), ((), ())),
                                preferred_element_type=F32)
            sp = jnp.maximum(z, 0.0) + jnp.log(1.0 + jnp.exp2(jnp.abs(z) * -LOG2_E))
            log_beta.append(z - sp)
            sp_sum = jnp.where(strictly_causal, sp, 0.0) if on_diagonal else sp
            hi = sp_sum.astype(BF16)
            lo = (sp_sum - hi.astype(F32)).astype(BF16)
            lhs.append(jnp.concatenate([hi, lo], axis=1))
        sums = jnp.dot(jnp.concatenate(lhs, axis=0), tri_ref[...],
                       preferred_element_type=F32)
        return starts, live, log_beta, sums

    def values(stage, c):
        starts, live, log_beta, sums = stage
        tail = sums[:, :KEY_BLOCK] + c
        c_new = c + sums[:, KEY_BLOCK:]
        mass_left = None
        for m in range(n_sub):
            w = jnp.exp(log_beta[m] - tail[sub_rows[m], :])
            v_blk = v_ref[0, pl.ds(starts[m], KEY_BLOCK), :]
            if live[m].ndim == 0:
                v_blk = jnp.where(live[m], v_blk, jnp.zeros_like(v_blk))
            else:
                w = jnp.where(live[m], w, 0.0)
            acc_ref[sub_rows[m], :] += jnp.dot(w.astype(BF16), v_blk,
                                               preferred_element_type=F32)
            if live[m].ndim == 0:
                sub_min = jnp.min(c_new[sub_rows[m], :].reshape(-1, 8, KEY_BLOCK), axis=0)
                sub_min = jnp.where(live[m], sub_min, jnp.inf)
                mass_left = sub_min if mass_left is None else jnp.minimum(mass_left, sub_min)
        return c_new, (None if mass_left is None else jnp.min(mass_left))

    stages = [scores(r) for r in range(peeled_steps + 1)]
    c = jnp.zeros(c_ref.shape, F32)
    min_mass = jnp.float32(0.0)
    for stage in stages:
        c, stage_min = values(stage, c)
        min_mass = min_mass if stage_min is None else stage_min
    c_ref[...] = c

    def keep_walking(carry):
        r, min_mass = carry
        return jnp.logical_and(r <= first + n_sub - 1, min_mass <= STICK_EXHAUSTED)

    def walk(carry):
        r, _ = carry
        c_new, min_mass = values(scores(r), c_ref[...])
        c_ref[...] = c_new
        return r + 1, min_mass

    lax.while_loop(keep_walking, walk, (jnp.int32(peeled_steps + 1), min_mass))
    o_ref[0] = acc_ref[...].astype(o_ref.dtype)


def _suffix_sum_matrix():
    j = lax.broadcasted_iota(jnp.int32, (2 * KEY_BLOCK, 2 * KEY_BLOCK), 0) % KEY_BLOCK
    s = lax.broadcasted_iota(jnp.int32, (2 * KEY_BLOCK, 2 * KEY_BLOCK), 1)
    return jnp.where(jnp.logical_or(s >= KEY_BLOCK, j > s), 1.0, 0.0).astype(BF16)


def _attention(q, k, v, *, n_heads, n_sub, peeled_steps):
    batch, seq, _ = q.shape
    tq = n_sub * KEY_BLOCK
    q_spec = pl.BlockSpec((1, tq, HEAD_DIM), lambda b, h, i: (b, i, h))
    kv_spec = pl.BlockSpec((1, seq, HEAD_DIM), lambda b, h, i: (b, 0, h))
    return pl.pallas_call(
        functools.partial(_attn_kernel, n_sub=n_sub, peeled_steps=peeled_steps),
        grid=(batch, n_heads, seq // tq),
        in_specs=[
            q_spec, kv_spec, kv_spec,
            pl.BlockSpec((2 * KEY_BLOCK, 2 * KEY_BLOCK), lambda b, h, i: (0, 0)),
        ],
        out_specs=q_spec,
        out_shape=jax.ShapeDtypeStruct(q.shape, BF16),
        scratch_shapes=[pltpu.VMEM((tq, HEAD_DIM), F32),
                        pltpu.VMEM((tq, KEY_BLOCK), F32)],
        compiler_params=pltpu.CompilerParams(
            dimension_semantics=("parallel", "parallel", "arbitrary"),
            vmem_limit_bytes=V7X_VMEM_LIMIT_BYTES),
        name="attention",
    )(q, k, v, _suffix_sum_matrix())


def _out_proj_kernel(x_ref, ya_ref, yc_ref, ag_ref, w_ref, o_ref):
    y_attn = _rms_norm(ya_ref[...].astype(F32), ag_ref[...]).astype(BF16)
    mix = jnp.concatenate([y_attn, yc_ref[...]], axis=1)
    o_ref[...] = x_ref[...] + jnp.dot(mix, w_ref[...], preferred_element_type=F32)


def _out_proj(x, ya, yc, attn_gain, w_out, *, tm):
    n_tok, d_model = x.shape
    d_sec = ya.shape[1]
    sec = pl.BlockSpec((tm, d_sec), lambda i: (i, 0))
    return pl.pallas_call(
        _out_proj_kernel,
        grid=(n_tok // tm,),
        in_specs=[
            pl.BlockSpec((tm, d_model), lambda i: (i, 0)),
            sec, sec,
            pl.BlockSpec((1, d_sec), lambda i: (0, 0)),
            pl.BlockSpec((d_model, d_model), lambda i: (0, 0)),
        ],
        out_specs=pl.BlockSpec((tm, d_model), lambda i: (i, 0)),
        out_shape=jax.ShapeDtypeStruct((n_tok, d_model), F32),
        compiler_params=pltpu.CompilerParams(
            dimension_semantics=("parallel",),
            vmem_limit_bytes=V7X_VMEM_LIMIT_BYTES),
        name="out_proj",
    )(x, ya, yc, attn_gain, w_out)


def kernel(x, ffn1_norm, ffn1_w_gate, ffn1_w_up, ffn1_w_down, mix_norm, w_in, q_norm, k_norm, conv_w, conv_b, attn_out_norm, conv_out_norm, w_out, ffn2_norm, ffn2_w_gate, ffn2_w_up, ffn2_w_down):
    batch, seq, d_model = x.shape
    depth = w_in.shape[0]
    n_heads = (w_in.shape[2] // 6) // HEAD_DIM
    n_tok = batch * seq
    row = lambda v: v.reshape(1, -1)

    h = x.reshape(n_tok, d_model)
    for l in range(depth):
        h = _ffn(h, row(ffn1_norm[l]), ffn1_w_gate[l].astype(BF16),
                 ffn1_w_up[l].astype(BF16), ffn1_w_down[l].astype(BF16),
                 tm=1024, tf=512, out_chunk=512)
        q, k, v, yc = _in_proj(h, row(mix_norm[l]), w_in[l].astype(BF16),
                               row(q_norm[l]) * (1.0 / math.sqrt(HEAD_DIM)),
                               row(k_norm[l]), conv_w[l],
                               row(conv_b[l]), row(conv_out_norm[l]), tm=512, seq=seq)
        to_seq = lambda t: t.reshape(batch, seq, -1)
        ya = _attention(to_seq(q), to_seq(k), to_seq(v), n_heads=n_heads, n_sub=8,
                        peeled_steps=2)
        h = _out_proj(h, ya.reshape(n_tok, -1), yc, row(attn_out_norm[l]),
                      w_out[l].astype(BF16), tm=512)
        h = _ffn(h, row(ffn2_norm[l]), ffn2_w_gate[l].astype(BF16),
                 ffn2_w_up[l].astype(BF16), ffn2_w_down[l].astype(BF16),
                 tm=1024, tf=512, out_chunk=512)
    return h.reshape(batch, seq, d_model)
```

```python
import functools
import math

import jax
import jax.numpy as jnp
from jax import lax
from jax.experimental import pallas as pl
from jax.experimental.pallas import tpu as pltpu

F32 = jnp.float32
BF16 = jnp.bfloat16

EPS = 1e-6
LOG2_E = math.log2(math.e)
FFN_RES_WEIGHT = 0.5
HEAD_DIM = 128
CONV_WIDTH = 3
KEY_BLOCK = 128
CONV_HALO_ROWS = 8

STICK_EXHAUSTED = 110.0

V7X_VMEM_LIMIT_BYTES = 56 * 1024 * 1024


def _rms_norm(x, gain):
    ms = jnp.mean(x * x, axis=-1, keepdims=True)
    return x * lax.rsqrt(ms + EPS) * gain


def _ffn_kernel(x_hbm, g_ref, wg_ref, wu_ref, wd_ref, o_ref, x_buf, xn_ref, x_sem):
    i = pl.program_id(0)
    j = pl.program_id(1)
    tm = o_ref.shape[0]

    def x_copy(tile):
        rows = pl.ds(pl.multiple_of(tile * tm, tm), tm)
        return pltpu.make_async_copy(x_hbm.at[rows, :], x_buf, x_sem)

    @pl.when(j == 0)
    def _():
        @pl.when(i == 0)
        def _():
            x_copy(0).start()

        x_copy(i).wait()
        x = x_buf[...]
        xn_ref[...] = _rms_norm(x, g_ref[...]).astype(BF16)
        o_ref[...] = x

        @pl.when(i + 1 < pl.num_programs(0))
        def _():
            x_copy(i + 1).start()

    xn = xn_ref[...]
    hg = jnp.dot(xn, wg_ref[...], preferred_element_type=F32)
    hu = jnp.dot(xn, wu_ref[...], preferred_element_type=F32)
    act = (hg * jax.nn.sigmoid(hg) * hu * FFN_RES_WEIGHT).astype(BF16)
    o_ref[...] += jnp.dot(act, wd_ref[...], preferred_element_type=F32)


def _ffn(x, gain, wg, wu, wd, *, tm, tf):
    n_tok, d_model = x.shape
    d_ff = wg.shape[1]
    return pl.pallas_call(
        _ffn_kernel,
        grid=(n_tok // tm, d_ff // tf),
        in_specs=[
            pl.BlockSpec(memory_space=pl.ANY),
            pl.BlockSpec((1, d_model), lambda i, j: (0, 0)),
            pl.BlockSpec((d_model, tf), lambda i, j: (0, j)),
            pl.BlockSpec((d_model, tf), lambda i, j: (0, j)),
            pl.BlockSpec((tf, d_model), lambda i, j: (j, 0)),
        ],
        out_specs=pl.BlockSpec((tm, d_model), lambda i, j: (i, 0)),
        out_shape=jax.ShapeDtypeStruct((n_tok, d_model), F32),
        scratch_shapes=[pltpu.VMEM((tm, d_model), F32),
                        pltpu.VMEM((tm, d_model), BF16),
                        pltpu.SemaphoreType.DMA(())],
        compiler_params=pltpu.CompilerParams(
            dimension_semantics=("arbitrary", "arbitrary"),
            vmem_limit_bytes=V7X_VMEM_LIMIT_BYTES),
        name="ffn",
    )(x, gain, wg, wu, wd)


def _in_proj_kernel(x_ref, g_ref, w_ref, qg_ref, kg_ref, cw_ref, cb_ref, cg_ref,
                    q_ref, k_ref, v_ref, yc_ref, cbuf_ref, *, tiles_per_seq):
    tm = x_ref.shape[0]
    d_sec = yc_ref.shape[1]
    at_seq_start = pl.program_id(0) % tiles_per_seq == 0

    @pl.when(at_seq_start)
    def _():
        cbuf_ref[0:CONV_HALO_ROWS, :] = jnp.zeros((CONV_HALO_ROWS, d_sec), F32)

    @pl.when(jnp.logical_not(at_seq_start))
    def _():
        cbuf_ref[0:CONV_HALO_ROWS, :] = cbuf_ref[tm:tm + CONV_HALO_ROWS, :]

    xn = _rms_norm(x_ref[...], g_ref[...]).astype(BF16)

    def section(s):
        return jnp.dot(xn, w_ref[:, s * d_sec:(s + 1) * d_sec],
                       preferred_element_type=F32)

    def to_heads(a, out_ref, gain=None):
        for h in range(d_sec // HEAD_DIM):
            a_h = a[:, h * HEAD_DIM:(h + 1) * HEAD_DIM]
            out_ref[0, h] = (a_h if gain is None else _rms_norm(a_h, gain)).astype(BF16)

    gate = section(3)
    cbuf_ref[CONV_HALO_ROWS:, :] = section(4) * section(5)
    y = cb_ref[...]
    for i in range(CONV_WIDTH):
        off = CONV_HALO_ROWS - (CONV_WIDTH - 1) + i
        y = y + cbuf_ref[off:off + tm, :] * cw_ref[i:i + 1, :]
    yc_ref[...] = _rms_norm(gate * y, cg_ref[...]).astype(BF16)

    to_heads(section(0), q_ref, qg_ref[...])
    to_heads(section(1), k_ref, kg_ref[...])
    to_heads(section(2), v_ref)


def _in_proj(x, gain, w_in, q_gain, k_gain, conv_w, conv_b, conv_gain, *, tm, seq):
    n_tok, d_model = x.shape
    d_sec = w_in.shape[1] // 6
    n_heads, tiles_per_seq = d_sec // HEAD_DIM, seq // tm
    sec_out = pl.BlockSpec((tm, d_sec), lambda i: (i, 0))
    out_sds = jax.ShapeDtypeStruct((n_tok, d_sec), BF16)
    head_out = pl.BlockSpec((1, n_heads, tm, HEAD_DIM),
                            lambda i: (i // tiles_per_seq, 0, i % tiles_per_seq, 0))
    head_sds = jax.ShapeDtypeStruct((n_tok // seq, n_heads, seq, HEAD_DIM), BF16)
    head_vec = pl.BlockSpec((1, HEAD_DIM), lambda i: (0, 0))
    sec_vec = pl.BlockSpec((1, d_sec), lambda i: (0, 0))
    return pl.pallas_call(
        functools.partial(_in_proj_kernel, tiles_per_seq=tiles_per_seq),
        grid=(n_tok // tm,),
        in_specs=[
            pl.BlockSpec((tm, d_model), lambda i: (i, 0)),
            pl.BlockSpec((1, d_model), lambda i: (0, 0)),
            pl.BlockSpec(w_in.shape, lambda i: (0, 0), pipeline_mode=pl.Buffered(1)),
            head_vec, head_vec,
            pl.BlockSpec((CONV_WIDTH, d_sec), lambda i: (0, 0)),
            sec_vec, sec_vec,
        ],
        out_specs=[head_out] * 3 + [sec_out],
        out_shape=[head_sds] * 3 + [out_sds],
        scratch_shapes=[pltpu.VMEM((tm + CONV_HALO_ROWS, d_sec), F32)],
        compiler_params=pltpu.CompilerParams(
            dimension_semantics=("arbitrary",),
            vmem_limit_bytes=V7X_VMEM_LIMIT_BYTES),
        name="in_proj",
    )(x, gain, w_in, q_gain, k_gain, conv_w, conv_b, conv_gain)


def _attn_kernel(q_ref, k_ref, v_ref, tri_ref, o_ref, acc_ref, c_ref,
                 *, n_sub, peeled_steps):
    qi = pl.program_id(2)
    first = qi * n_sub
    acc_ref[...] = jnp.zeros_like(acc_ref)

    row = lax.broadcasted_iota(jnp.int32, (KEY_BLOCK, KEY_BLOCK), 0)
    col = lax.broadcasted_iota(jnp.int32, (KEY_BLOCK, KEY_BLOCK), 1)
    strictly_causal = col < row
    sub_rows = [slice(m * KEY_BLOCK, (m + 1) * KEY_BLOCK) for m in range(n_sub)]

    def scores(r):
        on_diagonal = isinstance(r, int) and r == 0
        starts, live, log_beta, lhs = [], [], [], []
        for m in range(n_sub):
            kb = first + m - r
            live.append(strictly_causal if on_diagonal else kb >= 0)
            start = pl.multiple_of(jnp.maximum(kb, 0) * KEY_BLOCK, KEY_BLOCK)
            starts.append(start)
            z = lax.dot_general(q_ref[0, 0, sub_rows[m], :],
                                k_ref[0, 0, pl.ds(start, KEY_BLOCK), :],
                                (((1,), (1,)), ((), ())),
                                preferred_element_type=F32)
            sp = jnp.maximum(z, 0.0) + jnp.log(1.0 + jnp.exp2(jnp.abs(z) * -LOG2_E))
            log_beta.append(z - sp)
            sp_sum = jnp.where(strictly_causal, sp, 0.0) if on_diagonal else sp
            hi = sp_sum.astype(BF16)
            lo = (sp_sum - hi.astype(F32)).astype(BF16)
            lhs.append(jnp.concatenate([hi, lo], axis=1))
        sums = jnp.dot(jnp.concatenate(lhs, axis=0), tri_ref[...],
                       preferred_element_type=F32)
        return starts, live, log_beta, sums

    def values(stage, c):
        starts, live, log_beta, sums = stage
        tail = sums[:, :KEY_BLOCK] + c
        c_new = c + sums[:, KEY_BLOCK:]
        mass_left = None
        for m in range(n_sub):
            w = jnp.exp(log_beta[m] - tail[sub_rows[m], :])
            v_blk = v_ref[0, 0, pl.ds(starts[m], KEY_BLOCK), :]
            if live[m].ndim == 0:
                v_blk = jnp.where(live[m], v_blk, jnp.zeros_like(v_blk))
            else:
                w = jnp.where(live[m], w, 0.0)
            acc_ref[sub_rows[m], :] += jnp.dot(w.astype(BF16), v_blk,
                                               preferred_element_type=F32)
            if live[m].ndim == 0:
                sub_min = jnp.min(c_new[sub_rows[m], :].reshape(-1, 8, KEY_BLOCK), axis=0)
                sub_min = jnp.where(live[m], sub_min, jnp.inf)
                mass_left = sub_min if mass_left is None else jnp.minimum(mass_left, sub_min)
        return c_new, (None if mass_left is None else jnp.min(mass_left))

    stages = [scores(r) for r in range(peeled_steps + 1)]
    c = jnp.zeros(c_ref.shape, F32)
    min_mass = jnp.float32(0.0)
    for stage in stages:
        c, stage_min = values(stage, c)
        min_mass = min_mass if stage_min is None else stage_min
    c_ref[...] = c

    def keep_walking(carry):
        r, min_mass = carry
        return jnp.logical_and(r <= first + n_sub - 1, min_mass <= STICK_EXHAUSTED)

    def walk(carry):
        r, _ = carry
        c_new, min_mass = values(scores(r), c_ref[...])
        c_ref[...] = c_new
        return r + 1, min_mass

    lax.while_loop(keep_walking, walk, (jnp.int32(peeled_steps + 1), min_mass))
    o_ref[0, 0] = acc_ref[...].astype(o_ref.dtype)


def _suffix_sum_matrix():
    j = lax.broadcasted_iota(jnp.int32, (2 * KEY_BLOCK, 2 * KEY_BLOCK), 0) % KEY_BLOCK
    s = lax.broadcasted_iota(jnp.int32, (2 * KEY_BLOCK, 2 * KEY_BLOCK), 1)
    return jnp.where(jnp.logical_or(s >= KEY_BLOCK, j > s), 1.0, 0.0).astype(BF16)


def _attention(q, k, v, *, n_sub, peeled_steps):
    batch, n_heads, seq, _ = q.shape
    tq = n_sub * KEY_BLOCK
    q_spec = pl.BlockSpec((1, 1, tq, HEAD_DIM), lambda b, h, i: (b, h, i, 0))
    kv_spec = pl.BlockSpec((1, 1, seq, HEAD_DIM), lambda b, h, i: (b, h, 0, 0))
    return pl.pallas_call(
        functools.partial(_attn_kernel, n_sub=n_sub, peeled_steps=peeled_steps),
        grid=(batch, n_heads, seq // tq),
        in_specs=[
            q_spec, kv_spec, kv_spec,
            pl.BlockSpec((2 * KEY_BLOCK, 2 * KEY_BLOCK), lambda b, h, i: (0, 0)),
        ],
        out_specs=q_spec,
        out_shape=jax.ShapeDtypeStruct(q.shape, BF16),
        scratch_shapes=[pltpu.VMEM((tq, HEAD_DIM), F32),
                        pltpu.VMEM((tq, KEY_BLOCK), F32)],
        compiler_params=pltpu.CompilerParams(
            dimension_semantics=("parallel", "parallel", "arbitrary"),
            vmem_limit_bytes=V7X_VMEM_LIMIT_BYTES),
        name="attention",
    )(q, k, v, _suffix_sum_matrix())


def _out_proj_kernel(x_ref, ya_ref, yc_ref, ag_ref, w_ref, o_ref):
    ya = jnp.concatenate([ya_ref[0, h] for h in range(ya_ref.shape[1])], axis=1)
    y_attn = _rms_norm(ya.astype(F32), ag_ref[...]).astype(BF16)
    mix = jnp.concatenate([y_attn, yc_ref[...]], axis=1)
    o_ref[...] = x_ref[...] + jnp.dot(mix, w_ref[...], preferred_element_type=F32)


def _out_proj(x, ya, yc, attn_gain, w_out, *, tm):
    n_tok, d_model = x.shape
    _, n_heads, seq, _ = ya.shape
    d_sec = yc.shape[1]
    tiles_per_seq = seq // tm
    sec = pl.BlockSpec((tm, d_sec), lambda i: (i, 0))
    return pl.pallas_call(
        _out_proj_kernel,
        grid=(n_tok // tm,),
        in_specs=[
            pl.BlockSpec((tm, d_model), lambda i: (i, 0)),
            pl.BlockSpec((1, n_heads, tm, HEAD_DIM),
                         lambda i: (i // tiles_per_seq, 0, i % tiles_per_seq, 0)),
            sec,
            pl.BlockSpec((1, d_sec), lambda i: (0, 0)),
            pl.BlockSpec((d_model, d_model), lambda i: (0, 0)),
        ],
        out_specs=pl.BlockSpec((tm, d_model), lambda i: (i, 0)),
        out_shape=jax.ShapeDtypeStruct((n_tok, d_model), F32),
        compiler_params=pltpu.CompilerParams(
            dimension_semantics=("parallel",),
            vmem_limit_bytes=V7X_VMEM_LIMIT_BYTES),
        name="out_proj",
    )(x, ya, yc, attn_gain, w_out)


def kernel(x, ffn1_norm, ffn1_w_gate, ffn1_w_up, ffn1_w_down, mix_norm, w_in, q_norm, k_norm, conv_w, conv_b, attn_out_norm, conv_out_norm, w_out, ffn2_norm, ffn2_w_gate, ffn2_w_up, ffn2_w_down):
    batch, seq, d_model = x.shape
    depth = w_in.shape[0]
    n_heads = (w_in.shape[2] // 6) // HEAD_DIM
    n_tok = batch * seq
    row = lambda v: v.reshape(1, -1)

    h = x.reshape(n_tok, d_model)
    for l in range(depth):
        h = _ffn(h, row(ffn1_norm[l]), ffn1_w_gate[l].astype(BF16),
                 ffn1_w_up[l].astype(BF16), ffn1_w_down[l].astype(BF16),
                 tm=1024, tf=512)
        q, k, v, yc = _in_proj(h, row(mix_norm[l]), w_in[l].astype(BF16),
                               row(q_norm[l]) * (1.0 / math.sqrt(HEAD_DIM)),
                               row(k_norm[l]), conv_w[l],
                               row(conv_b[l]), row(conv_out_norm[l]), tm=512, seq=seq)
        ya = _attention(q, k, v, n_sub=16, peeled_steps=2)
        h = _out_proj(h, ya, yc, row(attn_out_norm[l]), w_out[l].astype(BF16), tm=512)
        h = _ffn(h, row(ffn2_norm[l]), ffn2_w_gate[l].astype(BF16),
                 ffn2_w_up[l].astype(BF16), ffn2_w_down[l].astype(BF16),
                 tm=1024, tf=512)
    return h.reshape(batch, seq, d_model)
```

```python
import functools
import math

import jax
import jax.numpy as jnp
from jax import lax
from jax.experimental import pallas as pl
from jax.experimental.pallas import tpu as pltpu

F32 = jnp.float32
BF16 = jnp.bfloat16

EPS = 1e-6
LOG2_E = math.log2(math.e)
FFN_RES_WEIGHT = 0.5
HEAD_DIM = 128
CONV_WIDTH = 3
KEY_BLOCK = 128
CONV_HALO_ROWS = 8

STICK_EXHAUSTED = 110.0

V7X_VMEM_LIMIT_BYTES = 56 * 1024 * 1024


def _rms_norm(x, gain):
    ms = jnp.mean(x * x, axis=-1, keepdims=True)
    return x * lax.rsqrt(ms + EPS) * gain


def _ffn_kernel(x_hbm, g_ref, wg_ref, wu_ref, wd_ref, o_ref, x_buf, xn_ref, x_sem):
    i = pl.program_id(0)
    j = pl.program_id(1)
    tm = o_ref.shape[0]

    def x_copy(tile):
        rows = pl.ds(pl.multiple_of(tile * tm, tm), tm)
        return pltpu.make_async_copy(x_hbm.at[rows, :], x_buf, x_sem)

    @pl.when(j == 0)
    def _():
        @pl.when(i == 0)
        def _():
            x_copy(0).start()

        x_copy(i).wait()
        x = x_buf[...]
        xn_ref[...] = _rms_norm(x, g_ref[...]).astype(BF16)
        o_ref[...] = x

        @pl.when(i + 1 < pl.num_programs(0))
        def _():
            x_copy(i + 1).start()

    xn = xn_ref[...]
    half = wg_ref.shape[1] // 2
    acts = []
    for c in (slice(0, half), slice(half, 2 * half)):
        hg = jnp.dot(xn, wg_ref[:, c], preferred_element_type=F32)
        hu = jnp.dot(xn, wu_ref[:, c], preferred_element_type=F32)
        acts.append((hg * jax.nn.sigmoid(hg) * hu * FFN_RES_WEIGHT).astype(BF16))
    o_ref[...] += jnp.dot(jnp.concatenate(acts, axis=1), wd_ref[...],
                          preferred_element_type=F32)


def _ffn(x, gain, wg, wu, wd, *, tm, tf):
    n_tok, d_model = x.shape
    d_ff = wg.shape[1]
    n_ff = d_ff // tf

    def column_tiles(w):
        return w.astype(BF16).reshape(d_model, n_ff, tf).transpose(1, 0, 2)

    col_tile = pl.BlockSpec((None, d_model, tf), lambda i, j: (j, 0, 0))
    return pl.pallas_call(
        _ffn_kernel,
        grid=(n_tok // tm, n_ff),
        in_specs=[
            pl.BlockSpec(memory_space=pl.ANY),
            pl.BlockSpec((1, d_model), lambda i, j: (0, 0)),
            col_tile, col_tile,
            pl.BlockSpec((tf, d_model), lambda i, j: (j, 0)),
        ],
        out_specs=pl.BlockSpec((tm, d_model), lambda i, j: (i, 0)),
        out_shape=jax.ShapeDtypeStruct((n_tok, d_model), F32),
        scratch_shapes=[pltpu.VMEM((tm, d_model), F32),
                        pltpu.VMEM((tm, d_model), BF16),
                        pltpu.SemaphoreType.DMA(())],
        compiler_params=pltpu.CompilerParams(
            dimension_semantics=("arbitrary", "arbitrary"),
            vmem_limit_bytes=V7X_VMEM_LIMIT_BYTES),
        name="ffn",
    )(x, gain, column_tiles(wg), column_tiles(wu), wd.astype(BF16))


def _in_proj_kernel(x_ref, g_ref, w_ref, qg_ref, kg_ref, cw_ref, cb_ref, cg_ref,
                    q_ref, k_ref, v_ref, yc_ref, cbuf_ref, *, tiles_per_seq):
    tm = x_ref.shape[0]
    d_sec = yc_ref.shape[1]
    at_seq_start = pl.program_id(0) % tiles_per_seq == 0

    @pl.when(at_seq_start)
    def _():
        cbuf_ref[0:CONV_HALO_ROWS, :] = jnp.zeros((CONV_HALO_ROWS, d_sec), F32)

    @pl.when(jnp.logical_not(at_seq_start))
    def _():
        cbuf_ref[0:CONV_HALO_ROWS, :] = cbuf_ref[tm:tm + CONV_HALO_ROWS, :]

    xn = _rms_norm(x_ref[...], g_ref[...]).astype(BF16)

    def section(s):
        return jnp.dot(xn, w_ref[:, s * d_sec:(s + 1) * d_sec],
                       preferred_element_type=F32)

    def to_heads(a, out_ref, gain=None):
        for h in range(d_sec // HEAD_DIM):
            a_h = a[:, h * HEAD_DIM:(h + 1) * HEAD_DIM]
            out_ref[0, h] = (a_h if gain is None else _rms_norm(a_h, gain)).astype(BF16)

    cbuf_ref[CONV_HALO_ROWS:, :] = section(4) * section(5)
    y = cb_ref[...]
    for i in range(CONV_WIDTH):
        off = CONV_HALO_ROWS - (CONV_WIDTH - 1) + i
        y = y + cbuf_ref[off:off + tm, :] * cw_ref[i:i + 1, :]
    yc_ref[...] = _rms_norm(section(3) * y, cg_ref[...]).astype(BF16)

    to_heads(section(0), q_ref, qg_ref[...])
    to_heads(section(1), k_ref, kg_ref[...])
    to_heads(section(2), v_ref)


def _in_proj(x, gain, w_in, q_gain, k_gain, conv_w, conv_b, conv_gain, *, tm, seq):
    n_tok, d_model = x.shape
    d_sec = w_in.shape[1] // 6
    n_heads, tiles_per_seq = d_sec // HEAD_DIM, seq // tm
    sec_out = pl.BlockSpec((tm, d_sec), lambda i: (i, 0))
    out_sds = jax.ShapeDtypeStruct((n_tok, d_sec), BF16)
    head_out = pl.BlockSpec((1, n_heads, tm, HEAD_DIM),
                            lambda i: (i // tiles_per_seq, 0, i % tiles_per_seq, 0))
    head_sds = jax.ShapeDtypeStruct((n_tok // seq, n_heads, seq, HEAD_DIM), BF16)
    head_vec = pl.BlockSpec((1, HEAD_DIM), lambda i: (0, 0))
    sec_vec = pl.BlockSpec((1, d_sec), lambda i: (0, 0))
    return pl.pallas_call(
        functools.partial(_in_proj_kernel, tiles_per_seq=tiles_per_seq),
        grid=(n_tok // tm,),
        in_specs=[
            pl.BlockSpec((tm, d_model), lambda i: (i, 0)),
            pl.BlockSpec((1, d_model), lambda i: (0, 0)),
            pl.BlockSpec(w_in.shape, lambda i: (0, 0), pipeline_mode=pl.Buffered(1)),
            head_vec, head_vec,
            pl.BlockSpec((CONV_WIDTH, d_sec), lambda i: (0, 0)),
            sec_vec, sec_vec,
        ],
        out_specs=[head_out] * 3 + [sec_out],
        out_shape=[head_sds] * 3 + [out_sds],
        scratch_shapes=[pltpu.VMEM((tm + CONV_HALO_ROWS, d_sec), F32)],
        compiler_params=pltpu.CompilerParams(
            dimension_semantics=("arbitrary",),
            vmem_limit_bytes=V7X_VMEM_LIMIT_BYTES),
        name="in_proj",
    )(x, gain, w_in, q_gain, k_gain, conv_w, conv_b, conv_gain)


def _attn_kernel(q_ref, k_ref, v_ref, tri_ref, o_ref, acc_ref, c_ref,
                 *, n_sub, peeled_steps):
    qi = pl.program_id(2)
    first = qi * n_sub
    acc_ref[...] = jnp.zeros_like(acc_ref)

    row = lax.broadcasted_iota(jnp.int32, (KEY_BLOCK, KEY_BLOCK), 0)
    col = lax.broadcasted_iota(jnp.int32, (KEY_BLOCK, KEY_BLOCK), 1)
    strictly_causal = col < row
    sub_rows = [slice(m * KEY_BLOCK, (m + 1) * KEY_BLOCK) for m in range(n_sub)]

    def scores(r):
        on_diagonal = isinstance(r, int) and r == 0
        starts, live, log_beta, lhs = [], [], [], []
        for m in range(n_sub):
            kb = first + m - r
            live.append(strictly_causal if on_diagonal else kb >= 0)
            start = pl.multiple_of(jnp.maximum(kb, 0) * KEY_BLOCK, KEY_BLOCK)
            starts.append(start)
            z = lax.dot_general(q_ref[0, 0, sub_rows[m], :],
                                k_ref[0, 0, pl.ds(start, KEY_BLOCK), :],
                                (((1,), (1,)), ((), ())),
                                preferred_element_type=F32)
            sp = jnp.maximum(z, 0.0) + jnp.log(1.0 + jnp.exp2(jnp.abs(z) * -LOG2_E))
            log_beta.append(z - sp)
            sp_sum = jnp.where(strictly_causal, sp, 0.0) if on_diagonal else sp
            hi = sp_sum.astype(BF16)
            lo = (sp_sum - hi.astype(F32)).astype(BF16)
            lhs.append(jnp.concatenate([hi, lo], axis=1))
        sums = jnp.dot(jnp.concatenate(lhs, axis=0), tri_ref[...],
                       preferred_element_type=F32)
        return starts, live, log_beta, sums

    def values(stage, c):
        starts, live, log_beta, sums = stage
        tail = sums[:, :KEY_BLOCK] + c
        c_new = c + sums[:, KEY_BLOCK:]
        mass_left = None
        for m in range(n_sub):
            w = jnp.exp(log_beta[m] - tail[sub_rows[m], :])
            v_blk = v_ref[0, 0, pl.ds(starts[m], KEY_BLOCK), :]
            if live[m].ndim == 0:
                v_blk = jnp.where(live[m], v_blk, jnp.zeros_like(v_blk))
            else:
                w = jnp.where(live[m], w, 0.0)
            acc_ref[sub_rows[m], :] += jnp.dot(w.astype(BF16), v_blk,
                                               preferred_element_type=F32)
            if live[m].ndim == 0:
                sub_min = jnp.min(c_new[sub_rows[m], :].reshape(-1, 8, KEY_BLOCK), axis=0)
                sub_min = jnp.where(live[m], sub_min, jnp.inf)
                mass_left = sub_min if mass_left is None else jnp.minimum(mass_left, sub_min)
        return c_new, (None if mass_left is None else jnp.min(mass_left))

    stages = [scores(r) for r in range(peeled_steps + 1)]
    c = jnp.zeros(c_ref.shape, F32)
    min_mass = jnp.float32(0.0)
    for stage in stages:
        c, stage_min = values(stage, c)
        min_mass = min_mass if stage_min is None else stage_min
    c_ref[...] = c

    def keep_walking(carry):
        r, min_mass = carry
        return jnp.logical_and(r <= first + n_sub - 1, min_mass <= STICK_EXHAUSTED)

    def walk(carry):
        r, _ = carry
        c_new, min_mass = values(scores(r), c_ref[...])
        c_ref[...] = c_new
        return r + 1, min_mass

    lax.while_loop(keep_walking, walk, (jnp.int32(peeled_steps + 1), min_mass))
    o_ref[0, 0] = acc_ref[...].astype(o_ref.dtype)


def _suffix_sum_matrix():
    j = lax.broadcasted_iota(jnp.int32, (2 * KEY_BLOCK, 2 * KEY_BLOCK), 0) % KEY_BLOCK
    s = lax.broadcasted_iota(jnp.int32, (2 * KEY_BLOCK, 2 * KEY_BLOCK), 1)
    return jnp.where(jnp.logical_or(s >= KEY_BLOCK, j > s), 1.0, 0.0).astype(BF16)


def _attention(q, k, v, *, n_sub, peeled_steps):
    batch, n_heads, seq, _ = q.shape
    tq = n_sub * KEY_BLOCK
    q_spec = pl.BlockSpec((1, 1, tq, HEAD_DIM), lambda b, h, i: (b, h, i, 0))
    kv_spec = pl.BlockSpec((1, 1, seq, HEAD_DIM), lambda b, h, i: (b, h, 0, 0))
    return pl.pallas_call(
        functools.partial(_attn_kernel, n_sub=n_sub, peeled_steps=peeled_steps),
        grid=(batch, n_heads, seq // tq),
        in_specs=[
            q_spec, kv_spec, kv_spec,
            pl.BlockSpec((2 * KEY_BLOCK, 2 * KEY_BLOCK), lambda b, h, i: (0, 0)),
        ],
        out_specs=q_spec,
        out_shape=jax.ShapeDtypeStruct(q.shape, BF16),
        scratch_shapes=[pltpu.VMEM((tq, HEAD_DIM), F32),
                        pltpu.VMEM((tq, KEY_BLOCK), F32)],
        compiler_params=pltpu.CompilerParams(
            dimension_semantics=("parallel", "parallel", "arbitrary"),
            vmem_limit_bytes=V7X_VMEM_LIMIT_BYTES),
        name="attention",
    )(q, k, v, _suffix_sum_matrix())


def _out_proj_kernel(x_ref, ya_ref, yc_ref, ag_ref, w_ref, o_ref):
    ya = jnp.concatenate([ya_ref[0, h] for h in range(ya_ref.shape[1])], axis=1)
    y_attn = _rms_norm(ya.astype(F32), ag_ref[...]).astype(BF16)
    mix = jnp.concatenate([y_attn, yc_ref[...]], axis=1)
    o_ref[...] = x_ref[...] + jnp.dot(mix, w_ref[...], preferred_element_type=F32)


def _out_proj(x, ya, yc, attn_gain, w_out, *, tm):
    n_tok, d_model = x.shape
    _, n_heads, seq, _ = ya.shape
    d_sec = yc.shape[1]
    tiles_per_seq = seq // tm
    sec = pl.BlockSpec((tm, d_sec), lambda i: (i, 0))
    return pl.pallas_call(
        _out_proj_kernel,
        grid=(n_tok // tm,),
        in_specs=[
            pl.BlockSpec((tm, d_model), lambda i: (i, 0)),
            pl.BlockSpec((1, n_heads, tm, HEAD_DIM),
                         lambda i: (i // tiles_per_seq, 0, i % tiles_per_seq, 0)),
            sec,
            pl.BlockSpec((1, d_sec), lambda i: (0, 0)),
            pl.BlockSpec((d_model, d_model), lambda i: (0, 0)),
        ],
        out_specs=pl.BlockSpec((tm, d_model), lambda i: (i, 0)),
        out_shape=jax.ShapeDtypeStruct((n_tok, d_model), F32),
        compiler_params=pltpu.CompilerParams(
            dimension_semantics=("parallel",),
            vmem_limit_bytes=V7X_VMEM_LIMIT_BYTES),
        name="out_proj",
    )(x, ya, yc, attn_gain, w_out)


def kernel(x, ffn1_norm, ffn1_w_gate, ffn1_w_up, ffn1_w_down, mix_norm, w_in, q_norm, k_norm, conv_w, conv_b, attn_out_norm, conv_out_norm, w_out, ffn2_norm, ffn2_w_gate, ffn2_w_up, ffn2_w_down):
    batch, seq, d_model = x.shape
    depth = w_in.shape[0]
    n_heads = (w_in.shape[2] // 6) // HEAD_DIM
    n_tok = batch * seq
    row = lambda v: v.reshape(1, -1)

    h = x.reshape(n_tok, d_model)
    for l in range(depth):
        h = _ffn(h, row(ffn1_norm[l]), ffn1_w_gate[l], ffn1_w_up[l], ffn1_w_down[l],
                 tm=1024, tf=512)
        q, k, v, yc = _in_proj(h, row(mix_norm[l]), w_in[l].astype(BF16),
                               row(q_norm[l]) * (1.0 / math.sqrt(HEAD_DIM)),
                               row(k_norm[l]), conv_w[l],
                               row(conv_b[l]), row(conv_out_norm[l]), tm=512, seq=seq)
        ya = _attention(q, k, v, n_sub=16, peeled_steps=2)
        h = _out_proj(h, ya, yc, row(attn_out_norm[l]), w_out[l].astype(BF16), tm=512)
        h = _ffn(h, row(ffn2_norm[l]), ffn2_w_gate[l], ffn2_w_up[l], ffn2_w_down[l],
                 tm=1024, tf=512)
    return h.reshape(batch, seq, d_model)
```

```python
import functools
import math

import jax
import jax.numpy as jnp
from jax import lax
from jax.experimental import pallas as pl
from jax.experimental.pallas import tpu as pltpu

F32 = jnp.float32
BF16 = jnp.bfloat16

EPS = 1e-6
LOG2_E = math.log2(math.e)
FFN_RES_WEIGHT = 0.5
HEAD_DIM = 128
CONV_WIDTH = 3
KEY_BLOCK = 128
CONV_HALO_ROWS = 8

STICK_EXHAUSTED = 110.0

V7X_VMEM_LIMIT_BYTES = 56 * 1024 * 1024


def _rms_norm(x, gain):
    ms = jnp.mean(x * x, axis=-1, keepdims=True)
    return x * lax.rsqrt(ms + EPS) * gain


def _ffn_kernel(x_hbm, g_ref, wg_ref, wu_ref, wd_ref, o_ref, x_buf, xn_ref, x_sem):
    i = pl.program_id(0)
    j = pl.program_id(1)
    tm = o_ref.shape[0]

    def x_copy(tile):
        rows = pl.ds(pl.multiple_of(tile * tm, tm), tm)
        return pltpu.make_async_copy(x_hbm.at[rows, :], x_buf, x_sem)

    @pl.when(j == 0)
    def _():
        @pl.when(i == 0)
        def _():
            x_copy(0).start()

        x_copy(i).wait()
        x = x_buf[...]
        xn_ref[...] = _rms_norm(x, g_ref[...]).astype(BF16)
        o_ref[...] = x

        @pl.when(i + 1 < pl.num_programs(0))
        def _():
            x_copy(i + 1).start()

    xn = xn_ref[...]
    half = wg_ref.shape[1] // 2
    acts = []
    for c in (slice(0, half), slice(half, 2 * half)):
        hg = jnp.dot(xn, wg_ref[:, c], preferred_element_type=F32)
        hu = jnp.dot(xn, wu_ref[:, c], preferred_element_type=F32)
        acts.append((hg * jax.nn.sigmoid(hg) * hu * FFN_RES_WEIGHT).astype(BF16))
    o_ref[...] += jnp.dot(jnp.concatenate(acts, axis=1), wd_ref[...],
                          preferred_element_type=F32)


def _ffn(x, gain, wg, wu, wd, *, tm, tf):
    n_tok, d_model = x.shape
    d_ff = wg.shape[1]
    col_tile = pl.BlockSpec((d_model, tf), lambda i, j: (0, j))
    return pl.pallas_call(
        _ffn_kernel,
        grid=(n_tok // tm, d_ff // tf),
        in_specs=[
            pl.BlockSpec(memory_space=pl.ANY),
            pl.BlockSpec((1, d_model), lambda i, j: (0, 0)),
            col_tile, col_tile,
            pl.BlockSpec((tf, d_model), lambda i, j: (j, 0)),
        ],
        out_specs=pl.BlockSpec((tm, d_model), lambda i, j: (i, 0)),
        out_shape=jax.ShapeDtypeStruct((n_tok, d_model), F32),
        scratch_shapes=[pltpu.VMEM((tm, d_model), F32),
                        pltpu.VMEM((tm, d_model), BF16),
                        pltpu.SemaphoreType.DMA(())],
        compiler_params=pltpu.CompilerParams(
            dimension_semantics=("arbitrary", "arbitrary"),
            vmem_limit_bytes=V7X_VMEM_LIMIT_BYTES),
        name="ffn",
    )(x, gain, wg.astype(BF16), wu.astype(BF16), wd.astype(BF16))


def _in_proj_kernel(x_ref, g_ref, w_ref, qg_ref, kg_ref, cw_ref, cb_ref, cg_ref,
                    q_ref, k_ref, v_ref, yc_ref, cbuf_ref, *, tiles_per_seq):
    tm = x_ref.shape[0]
    d_sec = yc_ref.shape[1]
    at_seq_start = pl.program_id(0) % tiles_per_seq == 0

    @pl.when(at_seq_start)
    def _():
        cbuf_ref[0:CONV_HALO_ROWS, :] = jnp.zeros((CONV_HALO_ROWS, d_sec), F32)

    @pl.when(jnp.logical_not(at_seq_start))
    def _():
        cbuf_ref[0:CONV_HALO_ROWS, :] = cbuf_ref[tm:tm + CONV_HALO_ROWS, :]

    xn = _rms_norm(x_ref[...], g_ref[...]).astype(BF16)

    def section(s):
        return jnp.dot(xn, w_ref[:, s * d_sec:(s + 1) * d_sec],
                       preferred_element_type=F32)

    def to_heads(a, out_ref, gain=None):
        for h in range(d_sec // HEAD_DIM):
            a_h = a[:, h * HEAD_DIM:(h + 1) * HEAD_DIM]
            out_ref[0, h] = (a_h if gain is None else _rms_norm(a_h, gain)).astype(BF16)

    cbuf_ref[CONV_HALO_ROWS:, :] = section(4) * section(5)
    y = cb_ref[...]
    for i in range(CONV_WIDTH):
        off = CONV_HALO_ROWS - (CONV_WIDTH - 1) + i
        y = y + cbuf_ref[off:off + tm, :] * cw_ref[i:i + 1, :]
    yc_ref[...] = _rms_norm(section(3) * y, cg_ref[...]).astype(BF16)

    to_heads(section(0), q_ref, qg_ref[...])
    to_heads(section(1), k_ref, kg_ref[...])
    to_heads(section(2), v_ref)


def _in_proj(x, gain, w_in, q_gain, k_gain, conv_w, conv_b, conv_gain, *, tm, seq):
    n_tok, d_model = x.shape
    d_sec = w_in.shape[1] // 6
    n_heads, tiles_per_seq = d_sec // HEAD_DIM, seq // tm
    sec_out = pl.BlockSpec((tm, d_sec), lambda i: (i, 0))
    out_sds = jax.ShapeDtypeStruct((n_tok, d_sec), BF16)
    head_out = pl.BlockSpec((1, n_heads, tm, HEAD_DIM),
                            lambda i: (i // tiles_per_seq, 0, i % tiles_per_seq, 0))
    head_sds = jax.ShapeDtypeStruct((n_tok // seq, n_heads, seq, HEAD_DIM), BF16)
    head_vec = pl.BlockSpec((1, HEAD_DIM), lambda i: (0, 0))
    sec_vec = pl.BlockSpec((1, d_sec), lambda i: (0, 0))
    return pl.pallas_call(
        functools.partial(_in_proj_kernel, tiles_per_seq=tiles_per_seq),
        grid=(n_tok // tm,),
        in_specs=[
            pl.BlockSpec((tm, d_model), lambda i: (i, 0)),
            pl.BlockSpec((1, d_model), lambda i: (0, 0)),
            pl.BlockSpec(w_in.shape, lambda i: (0, 0), pipeline_mode=pl.Buffered(1)),
            head_vec, head_vec,
            pl.BlockSpec((CONV_WIDTH, d_sec), lambda i: (0, 0)),
            sec_vec, sec_vec,
        ],
        out_specs=[head_out] * 3 + [sec_out],
        out_shape=[head_sds] * 3 + [out_sds],
        scratch_shapes=[pltpu.VMEM((tm + CONV_HALO_ROWS, d_sec), F32)],
        compiler_params=pltpu.CompilerParams(
            dimension_semantics=("arbitrary",),
            vmem_limit_bytes=V7X_VMEM_LIMIT_BYTES),
        name="in_proj",
    )(x, gain, w_in, q_gain, k_gain, conv_w, conv_b, conv_gain)


def _attn_kernel(q_ref, k_ref, v_ref, tri_ref, o_ref, acc_ref, c_ref,
                 *, n_sub, peeled_steps):
    qi = pl.program_id(2)
    first = qi * n_sub
    acc_ref[...] = jnp.zeros_like(acc_ref)

    row = lax.broadcasted_iota(jnp.int32, (KEY_BLOCK, KEY_BLOCK), 0)
    col = lax.broadcasted_iota(jnp.int32, (KEY_BLOCK, KEY_BLOCK), 1)
    strictly_causal = col < row
    sub_rows = [slice(m * KEY_BLOCK, (m + 1) * KEY_BLOCK) for m in range(n_sub)]

    def scores(r):
        on_diagonal = isinstance(r, int) and r == 0
        starts, live, log_beta, lhs = [], [], [], []
        for m in range(n_sub):
            kb = first + m - r
            live.append(strictly_causal if on_diagonal else kb >= 0)
            start = pl.multiple_of(jnp.maximum(kb, 0) * KEY_BLOCK, KEY_BLOCK)
            starts.append(start)
            z = lax.dot_general(q_ref[0, 0, sub_rows[m], :],
                                k_ref[0, 0, pl.ds(start, KEY_BLOCK), :],
                                (((1,), (1,)), ((), ())),
                                preferred_element_type=F32)
            sp = jnp.maximum(z, 0.0) + jnp.log(1.0 + jnp.exp2(jnp.abs(z) * -LOG2_E))
            log_beta.append(z - sp)
            sp_sum = jnp.where(strictly_causal, sp, 0.0) if on_diagonal else sp
            hi = sp_sum.astype(BF16)
            lo = (sp_sum - hi.astype(F32)).astype(BF16)
            lhs.append(jnp.concatenate([hi, lo], axis=1))
        sums = jnp.dot(jnp.concatenate(lhs, axis=0), tri_ref[...],
                       preferred_element_type=F32)
        return starts, live, log_beta, sums

    def values(stage, c):
        starts, live, log_beta, sums = stage
        tail = sums[:, :KEY_BLOCK] + c
        c_new = c + sums[:, KEY_BLOCK:]
        mass_left = None
        for m in range(n_sub):
            w = jnp.exp(log_beta[m] - tail[sub_rows[m], :])
            v_blk = v_ref[0, 0, pl.ds(starts[m], KEY_BLOCK), :]
            if live[m].ndim == 0:
                v_blk = jnp.where(live[m], v_blk, jnp.zeros_like(v_blk))
            else:
                w = jnp.where(live[m], w, 0.0)
            acc_ref[sub_rows[m], :] += jnp.dot(w.astype(BF16), v_blk,
                                               preferred_element_type=F32)
            if live[m].ndim == 0:
                sub_min = jnp.min(c_new[sub_rows[m], :].reshape(-1, 8, KEY_BLOCK), axis=0)
                sub_min = jnp.where(live[m], sub_min, jnp.inf)
                mass_left = sub_min if mass_left is None else jnp.minimum(mass_left, sub_min)
        return c_new, (None if mass_left is None else jnp.min(mass_left))

    stages = [scores(r) for r in range(peeled_steps + 1)]
    c = jnp.zeros(c_ref.shape, F32)
    min_mass = jnp.float32(0.0)
    for stage in stages:
        c, stage_min = values(stage, c)
        min_mass = min_mass if stage_min is None else stage_min
    c_ref[...] = c

    def keep_walking(carry):
        r, min_mass = carry
        return jnp.logical_and(r <= first + n_sub - 1, min_mass <= STICK_EXHAUSTED)

    def walk(carry):
        r, _ = carry
        c_new, min_mass = values(scores(r), c_ref[...])
        c_ref[...] = c_new
        return r + 1, min_mass

    lax.while_loop(keep_walking, walk, (jnp.int32(peeled_steps + 1), min_mass))
    o_ref[0, 0] = acc_ref[...].astype(o_ref.dtype)


def _suffix_sum_matrix():
    j = lax.broadcasted_iota(jnp.int32, (2 * KEY_BLOCK, 2 * KEY_BLOCK), 0) % KEY_BLOCK
    s = lax.broadcasted_iota(jnp.int32, (2 * KEY_BLOCK, 2 * KEY_BLOCK), 1)
    return jnp.where(jnp.logical_or(s >= KEY_BLOCK, j > s), 1.0, 0.0).astype(BF16)


def _attention(q, k, v, *, n_sub, peeled_steps):
    batch, n_heads, seq, _ = q.shape
    tq = n_sub * KEY_BLOCK
    q_spec = pl.BlockSpec((1, 1, tq, HEAD_DIM), lambda b, h, i: (b, h, i, 0))
    kv_spec = pl.BlockSpec((1, 1, seq, HEAD_DIM), lambda b, h, i: (b, h, 0, 0))
    return pl.pallas_call(
        functools.partial(_attn_kernel, n_sub=n_sub, peeled_steps=peeled_steps),
        grid=(batch, n_heads, seq // tq),
        in_specs=[
            q_spec, kv_spec, kv_spec,
            pl.BlockSpec((2 * KEY_BLOCK, 2 * KEY_BLOCK), lambda b, h, i: (0, 0)),
        ],
        out_specs=q_spec,
        out_shape=jax.ShapeDtypeStruct(q.shape, BF16),
        scratch_shapes=[pltpu.VMEM((tq, HEAD_DIM), F32),
                        pltpu.VMEM((tq, KEY_BLOCK), F32)],
        compiler_params=pltpu.CompilerParams(
            dimension_semantics=("parallel", "parallel", "arbitrary"),
            vmem_limit_bytes=V7X_VMEM_LIMIT_BYTES),
        name="attention",
    )(q, k, v, _suffix_sum_matrix())


def _out_proj_kernel(x_ref, ya_ref, yc_ref, ag_ref, w_ref, o_ref):
    ya = jnp.concatenate([ya_ref[0, h] for h in range(ya_ref.shape[1])], axis=1)
    y_attn = _rms_norm(ya.astype(F32), ag_ref[...]).astype(BF16)
    mix = jnp.concatenate([y_attn, yc_ref[...]], axis=1)
    o_ref[...] = x_ref[...] + jnp.dot(mix, w_ref[...], preferred_element_type=F32)


def _out_proj(x, ya, yc, attn_gain, w_out, *, tm):
    n_tok, d_model = x.shape
    _, n_heads, seq, _ = ya.shape
    d_sec = yc.shape[1]
    tiles_per_seq = seq // tm
    sec = pl.BlockSpec((tm, d_sec), lambda i: (i, 0))
    return pl.pallas_call(
        _out_proj_kernel,
        grid=(n_tok // tm,),
        in_specs=[
            pl.BlockSpec((tm, d_model), lambda i: (i, 0)),
            pl.BlockSpec((1, n_heads, tm, HEAD_DIM),
                         lambda i: (i // tiles_per_seq, 0, i % tiles_per_seq, 0)),
            sec,
            pl.BlockSpec((1, d_sec), lambda i: (0, 0)),
            pl.BlockSpec((d_model, d_model), lambda i: (0, 0)),
        ],
        out_specs=pl.BlockSpec((tm, d_model), lambda i: (i, 0)),
        out_shape=jax.ShapeDtypeStruct((n_tok, d_model), F32),
        compiler_params=pltpu.CompilerParams(
            dimension_semantics=("parallel",),
            vmem_limit_bytes=V7X_VMEM_LIMIT_BYTES),
        name="out_proj",
    )(x, ya, yc, attn_gain, w_out)


def kernel(x, ffn1_norm, ffn1_w_gate, ffn1_w_up, ffn1_w_down, mix_norm, w_in, q_norm, k_norm, conv_w, conv_b, attn_out_norm, conv_out_norm, w_out, ffn2_norm, ffn2_w_gate, ffn2_w_up, ffn2_w_down):
    batch, seq, d_model = x.shape
    depth = w_in.shape[0]
    n_heads = (w_in.shape[2] // 6) // HEAD_DIM
    n_tok = batch * seq
    row = lambda v: v.reshape(1, -1)

    h = x.reshape(n_tok, d_model)
    for l in range(depth):
        h = _ffn(h, row(ffn1_norm[l]), ffn1_w_gate[l], ffn1_w_up[l], ffn1_w_down[l],
                 tm=1024, tf=512)
        q, k, v, yc = _in_proj(h, row(mix_norm[l]), w_in[l].astype(BF16),
                               row(q_norm[l]) * (1.0 / math.sqrt(HEAD_DIM)),
                               row(k_norm[l]), conv_w[l],
                               row(conv_b[l]), row(conv_out_norm[l]), tm=512, seq=seq)
        ya = _attention(q, k, v, n_sub=32, peeled_steps=2)
        h = _out_proj(h, ya, yc, row(attn_out_norm[l]), w_out[l].astype(BF16), tm=512)
        h = _ffn(h, row(ffn2_norm[l]), ffn2_w_gate[l], ffn2_w_up[l], ffn2_w_down[l],
                 tm=1024, tf=512)
    return h.reshape(batch, seq, d_model)
```

```python
import functools
import math

import jax
import jax.numpy as jnp
from jax import lax
from jax.experimental import pallas as pl
from jax.experimental.pallas import tpu as pltpu

F32 = jnp.float32
BF16 = jnp.bfloat16

EPS = 1e-6
LOG2_E = math.log2(math.e)
FFN_RES_WEIGHT = 0.5
HEAD_DIM = 128
BF16_SUBLANES = 16
CONV_WIDTH = 3
KEY_BLOCK = 128
CONV_HALO_ROWS = 8

STICK_EXHAUSTED = 110.0

V7X_VMEM_LIMIT_BYTES = 56 * 1024 * 1024


def _rms_norm(x, gain):
    ms = jnp.mean(x * x, axis=-1, keepdims=True)
    return x * lax.rsqrt(ms + EPS) * gain


def _ffn_kernel(x_hbm, g_ref, wg_ref, wu_ref, wd_ref, *refs, n_cast):
    cast_src, (o_ref, *cast_dst) = refs[:n_cast], refs[n_cast:2 * n_cast + 1]
    x_buf, xn_ref, x_sem = refs[2 * n_cast + 1:]
    i = pl.program_id(0)
    j = pl.program_id(1)
    tm = o_ref.shape[0]

    def x_copy(tile):
        rows = pl.ds(pl.multiple_of(tile * tm, tm), tm)
        return pltpu.make_async_copy(x_hbm.at[rows, :], x_buf, x_sem)

    @pl.when(j == 0)
    def _():
        @pl.when(i == 0)
        def _():
            x_copy(0).start()

        x_copy(i).wait()
        x = x_buf[...]
        xn_ref[...] = _rms_norm(x, g_ref[...]).astype(BF16)
        o_ref[...] = x

        @pl.when(i + 1 < pl.num_programs(0))
        def _():
            x_copy(i + 1).start()

    xn = xn_ref[...]
    half = wg_ref.shape[1] // 2
    acts = []
    for c in (slice(0, half), slice(half, 2 * half)):
        hg = jnp.dot(xn, wg_ref[:, c], preferred_element_type=F32)
        hu = jnp.dot(xn, wu_ref[:, c], preferred_element_type=F32)
        acts.append((hg * jax.nn.sigmoid(hg) * hu * FFN_RES_WEIGHT).astype(BF16))
    o_ref[...] += jnp.dot(jnp.concatenate(acts, axis=1), wd_ref[...],
                          preferred_element_type=F32)

    for src, dst in zip(cast_src, cast_dst):
        dst[...] = src[...].astype(BF16)


def _ffn(x, gain, wg, wu, wd, *, tm, tf, cast_along=()):
    n_tok, d_model = x.shape
    d_ff = wg.shape[1]
    n_tiles, n_ff = n_tok // tm, d_ff // tf
    n_steps = n_tiles * n_ff

    def row_blocks(w):
        rows, cols = w.shape
        blk = BF16_SUBLANES * pl.cdiv(rows, BF16_SUBLANES * n_steps)
        assert rows % blk == 0
        last = rows // blk - 1
        return pl.BlockSpec((blk, cols), lambda i, j: (jnp.minimum(i * n_ff + j, last), 0))

    col_tile = pl.BlockSpec((d_model, tf), lambda i, j: (0, j))
    cast_specs = [row_blocks(w) for w in cast_along]
    out, *casts = pl.pallas_call(
        functools.partial(_ffn_kernel, n_cast=len(cast_along)),
        grid=(n_tiles, n_ff),
        in_specs=[
            pl.BlockSpec(memory_space=pl.ANY),
            pl.BlockSpec((1, d_model), lambda i, j: (0, 0)),
            col_tile, col_tile,
            pl.BlockSpec((tf, d_model), lambda i, j: (j, 0)),
            *cast_specs,
        ],
        out_specs=[pl.BlockSpec((tm, d_model), lambda i, j: (i, 0)), *cast_specs],
        out_shape=[jax.ShapeDtypeStruct((n_tok, d_model), F32),
                   *(jax.ShapeDtypeStruct(w.shape, BF16) for w in cast_along)],
        scratch_shapes=[pltpu.VMEM((tm, d_model), F32),
                        pltpu.VMEM((tm, d_model), BF16),
                        pltpu.SemaphoreType.DMA(())],
        compiler_params=pltpu.CompilerParams(
            dimension_semantics=("arbitrary", "arbitrary"),
            vmem_limit_bytes=V7X_VMEM_LIMIT_BYTES),
        name="ffn",
    )(x, gain, wg, wu, wd, *cast_along)
    return out, tuple(casts)


def _in_proj_kernel(x_ref, g_ref, w_ref, qg_ref, kg_ref, cw_ref, cb_ref, cg_ref,
                    q_ref, k_ref, v_ref, yc_ref, cbuf_ref, *, tiles_per_seq):
    tm = x_ref.shape[0]
    d_sec = yc_ref.shape[1]
    at_seq_start = pl.program_id(0) % tiles_per_seq == 0

    @pl.when(at_seq_start)
    def _():
        cbuf_ref[0:CONV_HALO_ROWS, :] = jnp.zeros((CONV_HALO_ROWS, d_sec), F32)

    @pl.when(jnp.logical_not(at_seq_start))
    def _():
        cbuf_ref[0:CONV_HALO_ROWS, :] = cbuf_ref[tm:tm + CONV_HALO_ROWS, :]

    xn = _rms_norm(x_ref[...], g_ref[...]).astype(BF16)

    def section(s):
        return jnp.dot(xn, w_ref[:, s * d_sec:(s + 1) * d_sec],
                       preferred_element_type=F32)

    def to_heads(a, out_ref, gain=None):
        for h in range(d_sec // HEAD_DIM):
            a_h = a[:, h * HEAD_DIM:(h + 1) * HEAD_DIM]
            out_ref[0, h] = (a_h if gain is None else _rms_norm(a_h, gain)).astype(BF16)

    cbuf_ref[CONV_HALO_ROWS:, :] = section(4) * section(5)
    y = cb_ref[...]
    for i in range(CONV_WIDTH):
        off = CONV_HALO_ROWS - (CONV_WIDTH - 1) + i
        y = y + cbuf_ref[off:off + tm, :] * cw_ref[i:i + 1, :]
    yc_ref[...] = _rms_norm(section(3) * y, cg_ref[...]).astype(BF16)

    to_heads(section(0), q_ref, qg_ref[...])
    to_heads(section(1), k_ref, kg_ref[...])
    to_heads(section(2), v_ref)


def _in_proj(x, gain, w_in, q_gain, k_gain, conv_w, conv_b, conv_gain, *, tm, seq):
    n_tok, d_model = x.shape
    d_sec = w_in.shape[1] // 6
    n_heads, tiles_per_seq = d_sec // HEAD_DIM, seq // tm
    sec_out = pl.BlockSpec((tm, d_sec), lambda i: (i, 0))
    out_sds = jax.ShapeDtypeStruct((n_tok, d_sec), BF16)
    head_out = pl.BlockSpec((1, n_heads, tm, HEAD_DIM),
                            lambda i: (i // tiles_per_seq, 0, i % tiles_per_seq, 0))
    head_sds = jax.ShapeDtypeStruct((n_tok // seq, n_heads, seq, HEAD_DIM), BF16)
    head_vec = pl.BlockSpec((1, HEAD_DIM), lambda i: (0, 0))
    sec_vec = pl.BlockSpec((1, d_sec), lambda i: (0, 0))
    return pl.pallas_call(
        functools.partial(_in_proj_kernel, tiles_per_seq=tiles_per_seq),
        grid=(n_tok // tm,),
        in_specs=[
            pl.BlockSpec((tm, d_model), lambda i: (i, 0)),
            pl.BlockSpec((1, d_model), lambda i: (0, 0)),
            pl.BlockSpec(w_in.shape, lambda i: (0, 0), pipeline_mode=pl.Buffered(1)),
            head_vec, head_vec,
            pl.BlockSpec((CONV_WIDTH, d_sec), lambda i: (0, 0)),
            sec_vec, sec_vec,
        ],
        out_specs=[head_out] * 3 + [sec_out],
        out_shape=[head_sds] * 3 + [out_sds],
        scratch_shapes=[pltpu.VMEM((tm + CONV_HALO_ROWS, d_sec), F32)],
        compiler_params=pltpu.CompilerParams(
            dimension_semantics=("arbitrary",),
            vmem_limit_bytes=V7X_VMEM_LIMIT_BYTES),
        name="in_proj",
    )(x, gain, w_in, q_gain, k_gain, conv_w, conv_b, conv_gain)


def _attn_kernel(q_ref, k_ref, v_ref, tri_ref, o_ref, acc_ref, c_ref,
                 *, n_sub, peeled_steps):
    qi = pl.program_id(2)
    first = qi * n_sub
    acc_ref[...] = jnp.zeros_like(acc_ref)

    row = lax.broadcasted_iota(jnp.int32, (KEY_BLOCK, KEY_BLOCK), 0)
    col = lax.broadcasted_iota(jnp.int32, (KEY_BLOCK, KEY_BLOCK), 1)
    strictly_causal = col < row
    sub_rows = [slice(m * KEY_BLOCK, (m + 1) * KEY_BLOCK) for m in range(n_sub)]

    def scores(r):
        on_diagonal = isinstance(r, int) and r == 0
        starts, live, log_beta, lhs = [], [], [], []
        for m in range(n_sub):
            kb = first + m - r
            live.append(strictly_causal if on_diagonal else kb >= 0)
            start = pl.multiple_of(jnp.maximum(kb, 0) * KEY_BLOCK, KEY_BLOCK)
            starts.append(start)
            z = lax.dot_general(q_ref[0, 0, sub_rows[m], :],
                                k_ref[0, 0, pl.ds(start, KEY_BLOCK), :],
                                (((1,), (1,)), ((), ())),
                                preferred_element_type=F32)
            sp = jnp.maximum(z, 0.0) + jnp.log(1.0 + jnp.exp2(jnp.abs(z) * -LOG2_E))
            log_beta.append(z - sp)
            sp_sum = jnp.where(strictly_causal, sp, 0.0) if on_diagonal else sp
            hi = sp_sum.astype(BF16)
            lo = (sp_sum - hi.astype(F32)).astype(BF16)
            lhs.append(jnp.concatenate([hi, lo], axis=1))
        sums = jnp.dot(jnp.concatenate(lhs, axis=0), tri_ref[...],
                       preferred_element_type=F32)
        return starts, live, log_beta, sums

    def values(stage, c):
        starts, live, log_beta, sums = stage
        tail = sums[:, :KEY_BLOCK] + c
        c_new = c + sums[:, KEY_BLOCK:]
        mass_left = None
        for m in range(n_sub):
            w = jnp.exp(log_beta[m] - tail[sub_rows[m], :])
            v_blk = v_ref[0, 0, pl.ds(starts[m], KEY_BLOCK), :]
            if live[m].ndim == 0:
                v_blk = jnp.where(live[m], v_blk, jnp.zeros_like(v_blk))
            else:
                w = jnp.where(live[m], w, 0.0)
            acc_ref[sub_rows[m], :] += jnp.dot(w.astype(BF16), v_blk,
                                               preferred_element_type=F32)
            if live[m].ndim == 0:
                sub_min = jnp.min(c_new[sub_rows[m], :].reshape(-1, 8, KEY_BLOCK), axis=0)
                sub_min = jnp.where(live[m], sub_min, jnp.inf)
                mass_left = sub_min if mass_left is None else jnp.minimum(mass_left, sub_min)
        return c_new, (None if mass_left is None else jnp.min(mass_left))

    stages = [scores(r) for r in range(peeled_steps + 1)]
    c = jnp.zeros(c_ref.shape, F32)
    min_mass = jnp.float32(0.0)
    for stage in stages:
        c, stage_min = values(stage, c)
        min_mass = min_mass if stage_min is None else stage_min
    c_ref[...] = c

    def keep_walking(carry):
        r, min_mass = carry
        return jnp.logical_and(r <= first + n_sub - 1, min_mass <= STICK_EXHAUSTED)

    def walk(carry):
        r, _ = carry
        c_new, min_mass = values(scores(r), c_ref[...])
        c_ref[...] = c_new
        return r + 1, min_mass

    lax.while_loop(keep_walking, walk, (jnp.int32(peeled_steps + 1), min_mass))
    o_ref[0, 0] = acc_ref[...].astype(o_ref.dtype)


def _suffix_sum_matrix():
    j = lax.broadcasted_iota(jnp.int32, (2 * KEY_BLOCK, 2 * KEY_BLOCK), 0) % KEY_BLOCK
    s = lax.broadcasted_iota(jnp.int32, (2 * KEY_BLOCK, 2 * KEY_BLOCK), 1)
    return jnp.where(jnp.logical_or(s >= KEY_BLOCK, j > s), 1.0, 0.0).astype(BF16)


def _attention(q, k, v, *, n_sub, peeled_steps):
    batch, n_heads, seq, _ = q.shape
    tq = n_sub * KEY_BLOCK
    q_spec = pl.BlockSpec((1, 1, tq, HEAD_DIM), lambda b, h, i: (b, h, i, 0))
    kv_spec = pl.BlockSpec((1, 1, seq, HEAD_DIM), lambda b, h, i: (b, h, 0, 0))
    return pl.pallas_call(
        functools.partial(_attn_kernel, n_sub=n_sub, peeled_steps=peeled_steps),
        grid=(batch, n_heads, seq // tq),
        in_specs=[
            q_spec, kv_spec, kv_spec,
            pl.BlockSpec((2 * KEY_BLOCK, 2 * KEY_BLOCK), lambda b, h, i: (0, 0)),
        ],
        out_specs=q_spec,
        out_shape=jax.ShapeDtypeStruct(q.shape, BF16),
        scratch_shapes=[pltpu.VMEM((tq, HEAD_DIM), F32),
                        pltpu.VMEM((tq, KEY_BLOCK), F32)],
        compiler_params=pltpu.CompilerParams(
            dimension_semantics=("parallel", "parallel", "arbitrary"),
            vmem_limit_bytes=V7X_VMEM_LIMIT_BYTES),
        name="attention",
    )(q, k, v, _suffix_sum_matrix())


def _out_proj_kernel(x_ref, ya_ref, yc_ref, ag_ref, w_ref, o_ref):
    ya = jnp.concatenate([ya_ref[0, h] for h in range(ya_ref.shape[1])], axis=1)
    y_attn = _rms_norm(ya.astype(F32), ag_ref[...]).astype(BF16)
    mix = jnp.concatenate([y_attn, yc_ref[...]], axis=1)
    o_ref[...] = x_ref[...] + jnp.dot(mix, w_ref[...], preferred_element_type=F32)


def _out_proj(x, ya, yc, attn_gain, w_out, *, tm):
    n_tok, d_model = x.shape
    _, n_heads, seq, _ = ya.shape
    d_sec = yc.shape[1]
    tiles_per_seq = seq // tm
    sec = pl.BlockSpec((tm, d_sec), lambda i: (i, 0))
    return pl.pallas_call(
        _out_proj_kernel,
        grid=(n_tok // tm,),
        in_specs=[
            pl.BlockSpec((tm, d_model), lambda i: (i, 0)),
            pl.BlockSpec((1, n_heads, tm, HEAD_DIM),
                         lambda i: (i // tiles_per_seq, 0, i % tiles_per_seq, 0)),
            sec,
            pl.BlockSpec((1, d_sec), lambda i: (0, 0)),
            pl.BlockSpec((d_model, d_model), lambda i: (0, 0)),
        ],
        out_specs=pl.BlockSpec((tm, d_model), lambda i: (i, 0)),
        out_shape=jax.ShapeDtypeStruct((n_tok, d_model), F32),
        compiler_params=pltpu.CompilerParams(
            dimension_semantics=("parallel",),
            vmem_limit_bytes=V7X_VMEM_LIMIT_BYTES),
        name="out_proj",
    )(x, ya, yc, attn_gain, w_out)


def kernel(x, ffn1_norm, ffn1_w_gate, ffn1_w_up, ffn1_w_down, mix_norm, w_in, q_norm, k_norm, conv_w, conv_b, attn_out_norm, conv_out_norm, w_out, ffn2_norm, ffn2_w_gate, ffn2_w_up, ffn2_w_down):
    batch, seq, d_model = x.shape
    depth = w_in.shape[0]
    n_heads = (w_in.shape[2] // 6) // HEAD_DIM
    n_tok = batch * seq
    row = lambda v: v.reshape(1, -1)

    h = x.reshape(n_tok, d_model)
    for l in range(depth):
        h, (w_in_b, w_out_b, wg2, wu2, wd2) = _ffn(
            h, row(ffn1_norm[l]), ffn1_w_gate[l].astype(BF16), ffn1_w_up[l].astype(BF16),
            ffn1_w_down[l].astype(BF16), tm=1024, tf=512,
            cast_along=(w_in[l], w_out[l], ffn2_w_gate[l], ffn2_w_up[l], ffn2_w_down[l]))
        q, k, v, yc = _in_proj(h, row(mix_norm[l]), w_in_b,
                               row(q_norm[l]) * (1.0 / math.sqrt(HEAD_DIM)),
                               row(k_norm[l]), conv_w[l],
                               row(conv_b[l]), row(conv_out_norm[l]), tm=512, seq=seq)
        ya = _attention(q, k, v, n_sub=32, peeled_steps=2)
        h = _out_proj(h, ya, yc, row(attn_out_norm[l]), w_out_b, tm=512)
        h, _ = _ffn(h, row(ffn2_norm[l]), wg2, wu2, wd2, tm=1024, tf=512)
    return h.reshape(batch, seq, d_model)
```

```python
import functools
import math

import jax
import jax.numpy as jnp
from jax import lax
from jax.experimental import pallas as pl
from jax.experimental.pallas import tpu as pltpu

F32 = jnp.float32
BF16 = jnp.bfloat16

EPS = 1e-6
LOG2_E = math.log2(math.e)
FFN_RES_WEIGHT = 0.5
HEAD_DIM = 128
BF16_SUBLANES = 16
CONV_WIDTH = 3
KEY_BLOCK = 128
CONV_HALO_ROWS = 8

STICK_EXHAUSTED = 110.0

V7X_VMEM_LIMIT_BYTES = 56 * 1024 * 1024


def _rms_norm(x, gain):
    ms = jnp.mean(x * x, axis=-1, keepdims=True)
    return x * lax.rsqrt(ms + EPS) * gain


def _ffn_kernel(x_hbm, g_ref, wg_ref, wu_ref, wd_ref, *refs, n_cast):
    cast_src, (o_ref, *cast_dst) = refs[:n_cast], refs[n_cast:2 * n_cast + 1]
    x_buf, xn_ref, x_sem = refs[2 * n_cast + 1:]
    i = pl.program_id(0)
    j = pl.program_id(1)
    tm = o_ref.shape[0]

    def x_copy(tile):
        rows = pl.ds(pl.multiple_of(tile * tm, tm), tm)
        return pltpu.make_async_copy(x_hbm.at[rows, :], x_buf, x_sem)

    @pl.when(j == 0)
    def _():
        @pl.when(i == 0)
        def _():
            x_copy(0).start()

        x_copy(i).wait()
        x = x_buf[...]
        xn_ref[...] = _rms_norm(x, g_ref[...]).astype(BF16)
        o_ref[...] = x

        @pl.when(i + 1 < pl.num_programs(0))
        def _():
            x_copy(i + 1).start()

    xn = xn_ref[...]
    half = wg_ref.shape[1] // 2
    acts = []
    for c in (slice(0, half), slice(half, 2 * half)):
        hg = jnp.dot(xn, wg_ref[:, c], preferred_element_type=F32)
        hu = jnp.dot(xn, wu_ref[:, c], preferred_element_type=F32)
        acts.append((hg * jax.nn.sigmoid(hg) * hu * FFN_RES_WEIGHT).astype(BF16))
    o_ref[...] += jnp.dot(jnp.concatenate(acts, axis=1), wd_ref[...],
                          preferred_element_type=F32)

    for src, dst in zip(cast_src, cast_dst):
        dst[...] = src[...].astype(BF16)


def _ffn(x, gain, wg, wu, wd, *, tm, tf, cast_along=()):
    n_tok, d_model = x.shape
    d_ff = wg.shape[1]
    n_tiles, n_ff = n_tok // tm, d_ff // tf
    n_steps = n_tiles * n_ff

    def row_blocks(w):
        rows, cols = w.shape
        blk = BF16_SUBLANES * pl.cdiv(rows, BF16_SUBLANES * n_steps)
        assert rows % blk == 0
        last = rows // blk - 1
        return pl.BlockSpec((blk, cols), lambda i, j: (jnp.minimum(i * n_ff + j, last), 0))

    col_tile = pl.BlockSpec((d_model, tf), lambda i, j: (0, j))
    cast_specs = [row_blocks(w) for w in cast_along]
    out, *casts = pl.pallas_call(
        functools.partial(_ffn_kernel, n_cast=len(cast_along)),
        grid=(n_tiles, n_ff),
        in_specs=[
            pl.BlockSpec(memory_space=pl.ANY),
            pl.BlockSpec((1, d_model), lambda i, j: (0, 0)),
            col_tile, col_tile,
            pl.BlockSpec((tf, d_model), lambda i, j: (j, 0)),
            *cast_specs,
        ],
        out_specs=[pl.BlockSpec((tm, d_model), lambda i, j: (i, 0)), *cast_specs],
        out_shape=[jax.ShapeDtypeStruct((n_tok, d_model), F32),
                   *(jax.ShapeDtypeStruct(w.shape, BF16) for w in cast_along)],
        scratch_shapes=[pltpu.VMEM((tm, d_model), F32),
                        pltpu.VMEM((tm, d_model), BF16),
                        pltpu.SemaphoreType.DMA(())],
        compiler_params=pltpu.CompilerParams(
            dimension_semantics=("arbitrary", "arbitrary"),
            vmem_limit_bytes=V7X_VMEM_LIMIT_BYTES),
        name="ffn",
    )(x, gain, wg, wu, wd, *cast_along)
    return out, tuple(casts)


def _in_proj_kernel(x_ref, g_ref, w_ref, qg_ref, kg_ref, cw_ref, cb_ref, cg_ref,
                    q_ref, k_ref, v_ref, yc_ref, cbuf_ref, *, tiles_per_seq):
    tm = x_ref.shape[0]
    d_sec = yc_ref.shape[1]
    at_seq_start = pl.program_id(0) % tiles_per_seq == 0

    @pl.when(at_seq_start)
    def _():
        cbuf_ref[0:CONV_HALO_ROWS, :] = jnp.zeros((CONV_HALO_ROWS, d_sec), F32)

    @pl.when(jnp.logical_not(at_seq_start))
    def _():
        cbuf_ref[0:CONV_HALO_ROWS, :] = cbuf_ref[tm:tm + CONV_HALO_ROWS, :]

    xn = _rms_norm(x_ref[...], g_ref[...]).astype(BF16)

    def section(s):
        return jnp.dot(xn, w_ref[:, s * d_sec:(s + 1) * d_sec],
                       preferred_element_type=F32)

    def to_heads(a, out_ref, gain=None):
        for h in range(d_sec // HEAD_DIM):
            a_h = a[:, h * HEAD_DIM:(h + 1) * HEAD_DIM]
            out_ref[0, h] = (a_h if gain is None else _rms_norm(a_h, gain)).astype(BF16)

    cbuf_ref[CONV_HALO_ROWS:, :] = section(4) * section(5)
    y = cb_ref[...]
    for i in range(CONV_WIDTH):
        off = CONV_HALO_ROWS - (CONV_WIDTH - 1) + i
        y = y + cbuf_ref[off:off + tm, :] * cw_ref[i:i + 1, :]
    yc_ref[...] = _rms_norm(section(3) * y, cg_ref[...]).astype(BF16)

    to_heads(section(0), q_ref, qg_ref[...])
    to_heads(section(1), k_ref, kg_ref[...])
    to_heads(section(2), v_ref)


def _in_proj(x, gain, w_in, q_gain, k_gain, conv_w, conv_b, conv_gain, *, tm, seq):
    n_tok, d_model = x.shape
    d_sec = w_in.shape[1] // 6
    n_heads, tiles_per_seq = d_sec // HEAD_DIM, seq // tm
    sec_out = pl.BlockSpec((tm, d_sec), lambda i: (i, 0))
    out_sds = jax.ShapeDtypeStruct((n_tok, d_sec), BF16)
    head_out = pl.BlockSpec((1, n_heads, tm, HEAD_DIM),
                            lambda i: (i // tiles_per_seq, 0, i % tiles_per_seq, 0))
    head_sds = jax.ShapeDtypeStruct((n_tok // seq, n_heads, seq, HEAD_DIM), BF16)
    head_vec = pl.BlockSpec((1, HEAD_DIM), lambda i: (0, 0))
    sec_vec = pl.BlockSpec((1, d_sec), lambda i: (0, 0))
    return pl.pallas_call(
        functools.partial(_in_proj_kernel, tiles_per_seq=tiles_per_seq),
        grid=(n_tok // tm,),
        in_specs=[
            pl.BlockSpec((tm, d_model), lambda i: (i, 0)),
            pl.BlockSpec((1, d_model), lambda i: (0, 0)),
            pl.BlockSpec(w_in.shape, lambda i: (0, 0), pipeline_mode=pl.Buffered(1)),
            head_vec, head_vec,
            pl.BlockSpec((CONV_WIDTH, d_sec), lambda i: (0, 0)),
            sec_vec, sec_vec,
        ],
        out_specs=[head_out] * 3 + [sec_out],
        out_shape=[head_sds] * 3 + [out_sds],
        scratch_shapes=[pltpu.VMEM((tm + CONV_HALO_ROWS, d_sec), F32)],
        compiler_params=pltpu.CompilerParams(
            dimension_semantics=("arbitrary",),
            vmem_limit_bytes=V7X_VMEM_LIMIT_BYTES),
        name="in_proj",
    )(x, gain, w_in, q_gain, k_gain, conv_w, conv_b, conv_gain)


def _attn_kernel(q_ref, k_ref, v_ref, tri_ref, o_ref, acc_ref, c_ref,
                 *, n_sub, head_rows):
    qi = pl.program_id(2)
    first = qi * n_sub
    acc_ref[...] = jnp.zeros_like(acc_ref)

    row = lax.broadcasted_iota(jnp.int32, (KEY_BLOCK, KEY_BLOCK), 0)
    col = lax.broadcasted_iota(jnp.int32, (KEY_BLOCK, KEY_BLOCK), 1)
    strictly_causal = col < row

    def rows_of(m, lo, hi):
        return slice(m * KEY_BLOCK + lo, m * KEY_BLOCK + hi)

    def scores(r, lo=0, hi=KEY_BLOCK):
        on_diagonal = isinstance(r, int) and r == 0
        starts, live, log_beta, lhs = [], [], [], []
        for m in range(n_sub):
            kb = first + m - r
            live.append(strictly_causal[lo:hi] if on_diagonal else kb >= 0)
            start = pl.multiple_of(jnp.maximum(kb, 0) * KEY_BLOCK, KEY_BLOCK)
            starts.append(start)
            z = lax.dot_general(q_ref[0, 0, rows_of(m, lo, hi), :],
                                k_ref[0, 0, pl.ds(start, KEY_BLOCK), :],
                                (((1,), (1,)), ((), ())),
                                preferred_element_type=F32)
            sp = jnp.maximum(z, 0.0) + jnp.log(1.0 + jnp.exp2(jnp.abs(z) * -LOG2_E))
            log_beta.append(z - sp)
            sp_sum = jnp.where(live[m], sp, 0.0) if on_diagonal else sp
            hi_part = sp_sum.astype(BF16)
            lo_part = (sp_sum - hi_part.astype(F32)).astype(BF16)
            lhs.append(jnp.concatenate([hi_part, lo_part], axis=1))
        sums = jnp.dot(jnp.concatenate(lhs, axis=0), tri_ref[...],
                       preferred_element_type=F32)
        return starts, live, log_beta, sums, (lo, hi)

    def values(stage, c):
        starts, live, log_beta, sums, (lo, hi) = stage
        n = hi - lo
        whole = n == KEY_BLOCK
        c_rows = c if whole else jnp.concatenate(
            [c[rows_of(m, lo, hi), :] for m in range(n_sub)], axis=0)
        tail = sums[:, :KEY_BLOCK] + c_rows
        c_rows = c_rows + sums[:, KEY_BLOCK:]
        for m in range(n_sub):
            w = jnp.exp(log_beta[m] - tail[m * n:(m + 1) * n, :])
            v_blk = v_ref[0, 0, pl.ds(starts[m], KEY_BLOCK), :]
            if live[m].ndim == 0:
                v_blk = jnp.where(live[m], v_blk, jnp.zeros_like(v_blk))
            else:
                w = jnp.where(live[m], w, 0.0)
            acc_ref[rows_of(m, lo, hi), :] += jnp.dot(w.astype(BF16), v_blk,
                                                      preferred_element_type=F32)
        if whole:
            return c_rows
        pieces = []
        for m in range(n_sub):
            pieces += [c[rows_of(m, 0, lo), :], c_rows[m * n:(m + 1) * n, :],
                       c[rows_of(m, hi, KEY_BLOCK), :]]
        return jnp.concatenate([p for p in pieces if p.shape[0]], axis=0)

    def min_mass(c, r, lo=0, hi=KEY_BLOCK):
        smallest = None
        for m in range(n_sub):
            sub = jnp.min(c[rows_of(m, lo, hi), :].reshape(-1, 8, KEY_BLOCK), axis=0)
            sub = jnp.where(first + m - r >= 0, sub, jnp.inf)
            smallest = sub if smallest is None else jnp.minimum(smallest, sub)
        return jnp.min(smallest)

    stages = [scores(0), scores(1), scores(2, 0, head_rows)]
    c = values(stages[1], values(stages[0], jnp.zeros(c_ref.shape, F32)))
    tail_rows_left = min_mass(c, 2, head_rows, KEY_BLOCK) <= STICK_EXHAUSTED
    c_ref[...] = values(stages[2], c)

    @pl.when(tail_rows_left)
    def _():
        c_ref[...] = values(scores(2, head_rows, KEY_BLOCK), c_ref[...])

    def keep_walking(carry):
        r, smallest = carry
        return jnp.logical_and(r <= first + n_sub - 1, smallest <= STICK_EXHAUSTED)

    def walk(carry):
        r, _ = carry
        c_new = values(scores(r), c_ref[...])
        c_ref[...] = c_new
        return r + 1, min_mass(c_new, r)

    lax.while_loop(keep_walking, walk, (jnp.int32(3), min_mass(c_ref[...], 2)))
    o_ref[0, 0] = acc_ref[...].astype(o_ref.dtype)


def _suffix_sum_matrix():
    j = lax.broadcasted_iota(jnp.int32, (2 * KEY_BLOCK, 2 * KEY_BLOCK), 0) % KEY_BLOCK
    s = lax.broadcasted_iota(jnp.int32, (2 * KEY_BLOCK, 2 * KEY_BLOCK), 1)
    return jnp.where(jnp.logical_or(s >= KEY_BLOCK, j > s), 1.0, 0.0).astype(BF16)


def _attention(q, k, v, *, n_sub, head_rows):
    batch, n_heads, seq, _ = q.shape
    tq = n_sub * KEY_BLOCK
    q_spec = pl.BlockSpec((1, 1, tq, HEAD_DIM), lambda b, h, i: (b, h, i, 0))
    kv_spec = pl.BlockSpec((1, 1, seq, HEAD_DIM), lambda b, h, i: (b, h, 0, 0))
    return pl.pallas_call(
        functools.partial(_attn_kernel, n_sub=n_sub, head_rows=head_rows),
        grid=(batch, n_heads, seq // tq),
        in_specs=[
            q_spec, kv_spec, kv_spec,
            pl.BlockSpec((2 * KEY_BLOCK, 2 * KEY_BLOCK), lambda b, h, i: (0, 0)),
        ],
        out_specs=q_spec,
        out_shape=jax.ShapeDtypeStruct(q.shape, BF16),
        scratch_shapes=[pltpu.VMEM((tq, HEAD_DIM), F32),
                        pltpu.VMEM((tq, KEY_BLOCK), F32)],
        compiler_params=pltpu.CompilerParams(
            dimension_semantics=("parallel", "parallel", "arbitrary"),
            vmem_limit_bytes=V7X_VMEM_LIMIT_BYTES),
        name="attention",
    )(q, k, v, _suffix_sum_matrix())


def _out_proj_kernel(x_ref, ya_ref, yc_ref, ag_ref, w_ref, o_ref):
    ya = jnp.concatenate([ya_ref[0, h] for h in range(ya_ref.shape[1])], axis=1)
    y_attn = _rms_norm(ya.astype(F32), ag_ref[...]).astype(BF16)
    mix = jnp.concatenate([y_attn, yc_ref[...]], axis=1)
    o_ref[...] = x_ref[...] + jnp.dot(mix, w_ref[...], preferred_element_type=F32)


def _out_proj(x, ya, yc, attn_gain, w_out, *, tm):
    n_tok, d_model = x.shape
    _, n_heads, seq, _ = ya.shape
    d_sec = yc.shape[1]
    tiles_per_seq = seq // tm
    sec = pl.BlockSpec((tm, d_sec), lambda i: (i, 0))
    return pl.pallas_call(
        _out_proj_kernel,
        grid=(n_tok // tm,),
        in_specs=[
            pl.BlockSpec((tm, d_model), lambda i: (i, 0)),
            pl.BlockSpec((1, n_heads, tm, HEAD_DIM),
                         lambda i: (i // tiles_per_seq, 0, i % tiles_per_seq, 0)),
            sec,
            pl.BlockSpec((1, d_sec), lambda i: (0, 0)),
            pl.BlockSpec((d_model, d_model), lambda i: (0, 0)),
        ],
        out_specs=pl.BlockSpec((tm, d_model), lambda i: (i, 0)),
        out_shape=jax.ShapeDtypeStruct((n_tok, d_model), F32),
        compiler_params=pltpu.CompilerParams(
            dimension_semantics=("parallel",),
            vmem_limit_bytes=V7X_VMEM_LIMIT_BYTES),
        name="out_proj",
    )(x, ya, yc, attn_gain, w_out)


def kernel(x, ffn1_norm, ffn1_w_gate, ffn1_w_up, ffn1_w_down, mix_norm, w_in, q_norm, k_norm, conv_w, conv_b, attn_out_norm, conv_out_norm, w_out, ffn2_norm, ffn2_w_gate, ffn2_w_up, ffn2_w_down):
    batch, seq, d_model = x.shape
    depth = w_in.shape[0]
    n_heads = (w_in.shape[2] // 6) // HEAD_DIM
    n_tok = batch * seq
    row = lambda v: v.reshape(1, -1)

    h = x.reshape(n_tok, d_model)
    for l in range(depth):
        h, (w_in_b, w_out_b, wg2, wu2, wd2) = _ffn(
            h, row(ffn1_norm[l]), ffn1_w_gate[l].astype(BF16), ffn1_w_up[l].astype(BF16),
            ffn1_w_down[l].astype(BF16), tm=1024, tf=512,
            cast_along=(w_in[l], w_out[l], ffn2_w_gate[l], ffn2_w_up[l], ffn2_w_down[l]))
        q, k, v, yc = _in_proj(h, row(mix_norm[l]), w_in_b,
                               row(q_norm[l]) * (1.0 / math.sqrt(HEAD_DIM)),
                               row(k_norm[l]), conv_w[l],
                               row(conv_b[l]), row(conv_out_norm[l]), tm=512, seq=seq)
        ya = _attention(q, k, v, n_sub=32, head_rows=48)
        h = _out_proj(h, ya, yc, row(attn_out_norm[l]), w_out_b, tm=512)
        h, _ = _ffn(h, row(ffn2_norm[l]), wg2, wu2, wd2, tm=1024, tf=512)
    return h.reshape(batch, seq, d_model)
```

```python
import functools
import math

import jax
import jax.numpy as jnp
from jax import lax
from jax.experimental import pallas as pl
from jax.experimental.pallas import tpu as pltpu

F32 = jnp.float32
BF16 = jnp.bfloat16

EPS = 1e-6
LOG2_E = math.log2(math.e)
FFN_RES_WEIGHT = 0.5
HEAD_DIM = 128
BF16_SUBLANES = 16
CONV_WIDTH = 3
KEY_BLOCK = 128
CONV_HALO_ROWS = 8

STICK_EXHAUSTED = 110.0

V7X_VMEM_LIMIT_BYTES = 56 * 1024 * 1024


def _rms_norm(x, gain):
    ms = jnp.mean(x * x, axis=-1, keepdims=True)
    return x * lax.rsqrt(ms + EPS) * gain


def _side_cast_specs(weights, n_steps, step_of):
    specs = []
    for w in weights:
        rows, cols = w.shape
        blk = BF16_SUBLANES * pl.cdiv(rows, BF16_SUBLANES * n_steps)
        while rows % blk:
            blk += BF16_SUBLANES
        last = rows // blk - 1
        specs.append(pl.BlockSpec(
            (blk, cols), lambda *ids, last=last: (jnp.minimum(step_of(*ids), last), 0)))
    return specs


def _side_cast(srcs, dsts):
    for src, dst in zip(srcs, dsts):
        dst[...] = src[...].astype(BF16)


def _ffn_kernel(x_hbm, g_ref, wg_ref, wu_ref, wd_ref, *refs, n_cast):
    cast_src, (o_ref, *cast_dst) = refs[:n_cast], refs[n_cast:2 * n_cast + 1]
    x_buf, xn_ref, x_sem = refs[2 * n_cast + 1:]
    i = pl.program_id(0)
    j = pl.program_id(1)
    tm = o_ref.shape[0]

    def x_copy(tile):
        rows = pl.ds(pl.multiple_of(tile * tm, tm), tm)
        return pltpu.make_async_copy(x_hbm.at[rows, :], x_buf, x_sem)

    @pl.when(j == 0)
    def _():
        @pl.when(i == 0)
        def _():
            x_copy(0).start()

        x_copy(i).wait()
        x = x_buf[...]
        xn_ref[...] = _rms_norm(x, g_ref[...]).astype(BF16)
        o_ref[...] = x

        @pl.when(i + 1 < pl.num_programs(0))
        def _():
            x_copy(i + 1).start()

    xn = xn_ref[...]
    half = wg_ref.shape[1] // 2
    acts = []
    for c in (slice(0, half), slice(half, 2 * half)):
        hg = jnp.dot(xn, wg_ref[:, c], preferred_element_type=F32)
        hu = jnp.dot(xn, wu_ref[:, c], preferred_element_type=F32)
        acts.append((hg * jax.nn.sigmoid(hg) * hu * FFN_RES_WEIGHT).astype(BF16))
    o_ref[...] += jnp.dot(jnp.concatenate(acts, axis=1), wd_ref[...],
                          preferred_element_type=F32)

    _side_cast(cast_src, cast_dst)


def _ffn(x, gain, wg, wu, wd, *, tm, tf, cast_along=()):
    n_tok, d_model = x.shape
    d_ff = wg.shape[1]
    n_tiles, n_ff = n_tok // tm, d_ff // tf
    col_tile = pl.BlockSpec((d_model, tf), lambda i, j: (0, j))
    cast_specs = _side_cast_specs(cast_along, n_tiles * n_ff, lambda i, j: i * n_ff + j)
    out, *casts = pl.pallas_call(
        functools.partial(_ffn_kernel, n_cast=len(cast_along)),
        grid=(n_tiles, n_ff),
        in_specs=[
            pl.BlockSpec(memory_space=pl.ANY),
            pl.BlockSpec((1, d_model), lambda i, j: (0, 0)),
            col_tile, col_tile,
            pl.BlockSpec((tf, d_model), lambda i, j: (j, 0)),
            *cast_specs,
        ],
        out_specs=[pl.BlockSpec((tm, d_model), lambda i, j: (i, 0)), *cast_specs],
        out_shape=[jax.ShapeDtypeStruct((n_tok, d_model), F32),
                   *(jax.ShapeDtypeStruct(w.shape, BF16) for w in cast_along)],
        scratch_shapes=[pltpu.VMEM((tm, d_model), F32),
                        pltpu.VMEM((tm, d_model), BF16),
                        pltpu.SemaphoreType.DMA(())],
        compiler_params=pltpu.CompilerParams(
            dimension_semantics=("arbitrary", "arbitrary"),
            vmem_limit_bytes=V7X_VMEM_LIMIT_BYTES),
        name="ffn",
    )(x, gain, wg, wu, wd, *cast_along)
    return out, tuple(casts)


def _in_proj_kernel(x_ref, g_ref, w_ref, qg_ref, kg_ref, cw_ref, cb_ref, cg_ref,
                    q_ref, k_ref, v_ref, yc_ref, cbuf_ref, *, tiles_per_seq):
    tm = x_ref.shape[0]
    d_sec = yc_ref.shape[1]
    at_seq_start = pl.program_id(0) % tiles_per_seq == 0

    @pl.when(at_seq_start)
    def _():
        cbuf_ref[0:CONV_HALO_ROWS, :] = jnp.zeros((CONV_HALO_ROWS, d_sec), F32)

    @pl.when(jnp.logical_not(at_seq_start))
    def _():
        cbuf_ref[0:CONV_HALO_ROWS, :] = cbuf_ref[tm:tm + CONV_HALO_ROWS, :]

    xn = _rms_norm(x_ref[...], g_ref[...]).astype(BF16)

    def section(s):
        return jnp.dot(xn, w_ref[:, s * d_sec:(s + 1) * d_sec],
                       preferred_element_type=F32)

    def to_heads(a, out_ref, gain=None):
        for h in range(d_sec // HEAD_DIM):
            a_h = a[:, h * HEAD_DIM:(h + 1) * HEAD_DIM]
            out_ref[0, h] = (a_h if gain is None else _rms_norm(a_h, gain)).astype(BF16)

    cbuf_ref[CONV_HALO_ROWS:, :] = section(4) * section(5)
    y = cb_ref[...]
    for i in range(CONV_WIDTH):
        off = CONV_HALO_ROWS - (CONV_WIDTH - 1) + i
        y = y + cbuf_ref[off:off + tm, :] * cw_ref[i:i + 1, :]
    yc_ref[...] = _rms_norm(section(3) * y, cg_ref[...]).astype(BF16)

    to_heads(section(0), q_ref, qg_ref[...])
    to_heads(section(1), k_ref, kg_ref[...])
    to_heads(section(2), v_ref)


def _in_proj(x, gain, w_in, q_gain, k_gain, conv_w, conv_b, conv_gain, *, tm, seq):
    n_tok, d_model = x.shape
    d_sec = w_in.shape[1] // 6
    n_heads, tiles_per_seq = d_sec // HEAD_DIM, seq // tm
    sec_out = pl.BlockSpec((tm, d_sec), lambda i: (i, 0))
    out_sds = jax.ShapeDtypeStruct((n_tok, d_sec), BF16)
    head_out = pl.BlockSpec((1, n_heads, tm, HEAD_DIM),
                            lambda i: (i // tiles_per_seq, 0, i % tiles_per_seq, 0))
    head_sds = jax.ShapeDtypeStruct((n_tok // seq, n_heads, seq, HEAD_DIM), BF16)
    head_vec = pl.BlockSpec((1, HEAD_DIM), lambda i: (0, 0))
    sec_vec = pl.BlockSpec((1, d_sec), lambda i: (0, 0))
    return pl.pallas_call(
        functools.partial(_in_proj_kernel, tiles_per_seq=tiles_per_seq),
        grid=(n_tok // tm,),
        in_specs=[
            pl.BlockSpec((tm, d_model), lambda i: (i, 0)),
            pl.BlockSpec((1, d_model), lambda i: (0, 0)),
            pl.BlockSpec(w_in.shape, lambda i: (0, 0), pipeline_mode=pl.Buffered(1)),
            head_vec, head_vec,
            pl.BlockSpec((CONV_WIDTH, d_sec), lambda i: (0, 0)),
            sec_vec, sec_vec,
        ],
        out_specs=[head_out] * 3 + [sec_out],
        out_shape=[head_sds] * 3 + [out_sds],
        scratch_shapes=[pltpu.VMEM((tm + CONV_HALO_ROWS, d_sec), F32)],
        compiler_params=pltpu.CompilerParams(
            dimension_semantics=("arbitrary",),
            vmem_limit_bytes=V7X_VMEM_LIMIT_BYTES),
        name="in_proj",
    )(x, gain, w_in, q_gain, k_gain, conv_w, conv_b, conv_gain)


def _attn_kernel(q_ref, k_ref, v_ref, tri_ref, o_ref, acc_ref, c_ref,
                 *, n_sub, head_rows):
    qi = pl.program_id(2)
    first = qi * n_sub
    acc_ref[...] = jnp.zeros_like(acc_ref)

    row = lax.broadcasted_iota(jnp.int32, (KEY_BLOCK, KEY_BLOCK), 0)
    col = lax.broadcasted_iota(jnp.int32, (KEY_BLOCK, KEY_BLOCK), 1)
    strictly_causal = col < row

    def rows_of(m, lo, hi):
        return slice(m * KEY_BLOCK + lo, m * KEY_BLOCK + hi)

    def scores(r, lo=0, hi=KEY_BLOCK):
        on_diagonal = isinstance(r, int) and r == 0
        starts, live, log_beta, lhs = [], [], [], []
        for m in range(n_sub):
            kb = first + m - r
            live.append(strictly_causal[lo:hi] if on_diagonal else kb >= 0)
            start = pl.multiple_of(jnp.maximum(kb, 0) * KEY_BLOCK, KEY_BLOCK)
            starts.append(start)
            z = lax.dot_general(q_ref[0, 0, rows_of(m, lo, hi), :],
                                k_ref[0, 0, pl.ds(start, KEY_BLOCK), :],
                                (((1,), (1,)), ((), ())),
                                preferred_element_type=F32)
            sp = jnp.maximum(z, 0.0) + jnp.log(1.0 + jnp.exp2(jnp.abs(z) * -LOG2_E))
            log_beta.append(z - sp)
            sp_sum = jnp.where(live[m], sp, 0.0) if on_diagonal else sp
            hi_part = sp_sum.astype(BF16)
            lo_part = (sp_sum - hi_part.astype(F32)).astype(BF16)
            lhs.append(jnp.concatenate([hi_part, lo_part], axis=1))
        sums = jnp.dot(jnp.concatenate(lhs, axis=0), tri_ref[...],
                       preferred_element_type=F32)
        return starts, live, log_beta, sums, (lo, hi)

    def values(stage, c):
        starts, live, log_beta, sums, (lo, hi) = stage
        n = hi - lo
        whole = n == KEY_BLOCK
        c_rows = c if whole else jnp.concatenate(
            [c[rows_of(m, lo, hi), :] for m in range(n_sub)], axis=0)
        tail = sums[:, :KEY_BLOCK] + c_rows
        c_rows = c_rows + sums[:, KEY_BLOCK:]
        for m in range(n_sub):
            w = jnp.exp(log_beta[m] - tail[m * n:(m + 1) * n, :])
            v_blk = v_ref[0, 0, pl.ds(starts[m], KEY_BLOCK), :]
            if live[m].ndim == 0:
                v_blk = jnp.where(live[m], v_blk, jnp.zeros_like(v_blk))
            else:
                w = jnp.where(live[m], w, 0.0)
            acc_ref[rows_of(m, lo, hi), :] += jnp.dot(w.astype(BF16), v_blk,
                                                      preferred_element_type=F32)
        if whole:
            return c_rows
        pieces = []
        for m in range(n_sub):
            pieces += [c[rows_of(m, 0, lo), :], c_rows[m * n:(m + 1) * n, :],
                       c[rows_of(m, hi, KEY_BLOCK), :]]
        return jnp.concatenate([p for p in pieces if p.shape[0]], axis=0)

    def min_mass(c, r, lo=0, hi=KEY_BLOCK):
        smallest = None
        for m in range(n_sub):
            sub = jnp.min(c[rows_of(m, lo, hi), :].reshape(-1, 8, KEY_BLOCK), axis=0)
            sub = jnp.where(first + m - r >= 0, sub, jnp.inf)
            smallest = sub if smallest is None else jnp.minimum(smallest, sub)
        return jnp.min(smallest)

    stages = [scores(0), scores(1), scores(2, 0, head_rows)]
    c = values(stages[1], values(stages[0], jnp.zeros(c_ref.shape, F32)))
    tail_rows_left = min_mass(c, 2, head_rows, KEY_BLOCK) <= STICK_EXHAUSTED
    c_ref[...] = values(stages[2], c)

    @pl.when(tail_rows_left)
    def _():
        c_ref[...] = values(scores(2, head_rows, KEY_BLOCK), c_ref[...])

    def keep_walking(carry):
        r, smallest = carry
        return jnp.logical_and(r <= first + n_sub - 1, smallest <= STICK_EXHAUSTED)

    def walk(carry):
        r, _ = carry
        c_new = values(scores(r), c_ref[...])
        c_ref[...] = c_new
        return r + 1, min_mass(c_new, r)

    lax.while_loop(keep_walking, walk, (jnp.int32(3), min_mass(c_ref[...], 2)))
    o_ref[0, 0] = acc_ref[...].astype(o_ref.dtype)


def _suffix_sum_matrix():
    j = lax.broadcasted_iota(jnp.int32, (2 * KEY_BLOCK, 2 * KEY_BLOCK), 0) % KEY_BLOCK
    s = lax.broadcasted_iota(jnp.int32, (2 * KEY_BLOCK, 2 * KEY_BLOCK), 1)
    return jnp.where(jnp.logical_or(s >= KEY_BLOCK, j > s), 1.0, 0.0).astype(BF16)


def _attention(q, k, v, *, n_sub, head_rows):
    batch, n_heads, seq, _ = q.shape
    tq = n_sub * KEY_BLOCK
    q_spec = pl.BlockSpec((1, 1, tq, HEAD_DIM), lambda b, h, i: (b, h, i, 0))
    kv_spec = pl.BlockSpec((1, 1, seq, HEAD_DIM), lambda b, h, i: (b, h, 0, 0))
    return pl.pallas_call(
        functools.partial(_attn_kernel, n_sub=n_sub, head_rows=head_rows),
        grid=(batch, n_heads, seq // tq),
        in_specs=[
            q_spec, kv_spec, kv_spec,
            pl.BlockSpec((2 * KEY_BLOCK, 2 * KEY_BLOCK), lambda b, h, i: (0, 0)),
        ],
        out_specs=q_spec,
        out_shape=jax.ShapeDtypeStruct(q.shape, BF16),
        scratch_shapes=[pltpu.VMEM((tq, HEAD_DIM), F32),
                        pltpu.VMEM((tq, KEY_BLOCK), F32)],
        compiler_params=pltpu.CompilerParams(
            dimension_semantics=("parallel", "parallel", "arbitrary"),
            vmem_limit_bytes=V7X_VMEM_LIMIT_BYTES),
        name="attention",
    )(q, k, v, _suffix_sum_matrix())


def _out_proj_kernel(x_ref, ya_ref, yc_ref, ag_ref, w_ref, *refs, n_cast):
    cast_src, (o_ref, *cast_dst) = refs[:n_cast], refs[n_cast:]
    ya = jnp.concatenate([ya_ref[0, h] for h in range(ya_ref.shape[1])], axis=1)
    y_attn = _rms_norm(ya.astype(F32), ag_ref[...]).astype(BF16)
    mix = jnp.concatenate([y_attn, yc_ref[...]], axis=1)
    o_ref[...] = x_ref[...] + jnp.dot(mix, w_ref[...], preferred_element_type=F32)
    _side_cast(cast_src, cast_dst)


def _out_proj(x, ya, yc, attn_gain, w_out, *, tm, cast_along=()):
    n_tok, d_model = x.shape
    _, n_heads, seq, _ = ya.shape
    d_sec = yc.shape[1]
    tiles_per_seq = seq // tm
    sec = pl.BlockSpec((tm, d_sec), lambda i: (i, 0))
    cast_specs = _side_cast_specs(cast_along, n_tok // tm, lambda i: i)
    out, *casts = pl.pallas_call(
        functools.partial(_out_proj_kernel, n_cast=len(cast_along)),
        grid=(n_tok // tm,),
        in_specs=[
            pl.BlockSpec((tm, d_model), lambda i: (i, 0)),
            pl.BlockSpec((1, n_heads, tm, HEAD_DIM),
                         lambda i: (i // tiles_per_seq, 0, i % tiles_per_seq, 0)),
            sec,
            pl.BlockSpec((1, d_sec), lambda i: (0, 0)),
            pl.BlockSpec((d_model, d_model), lambda i: (0, 0)),
            *cast_specs,
        ],
        out_specs=[pl.BlockSpec((tm, d_model), lambda i: (i, 0)), *cast_specs],
        out_shape=[jax.ShapeDtypeStruct((n_tok, d_model), F32),
                   *(jax.ShapeDtypeStruct(w.shape, BF16) for w in cast_along)],
        compiler_params=pltpu.CompilerParams(
            dimension_semantics=("arbitrary",),
            vmem_limit_bytes=V7X_VMEM_LIMIT_BYTES),
        name="out_proj",
    )(x, ya, yc, attn_gain, w_out, *cast_along)
    return out, tuple(casts)


def kernel(x, ffn1_norm, ffn1_w_gate, ffn1_w_up, ffn1_w_down, mix_norm, w_in, q_norm, k_norm, conv_w, conv_b, attn_out_norm, conv_out_norm, w_out, ffn2_norm, ffn2_w_gate, ffn2_w_up, ffn2_w_down):
    batch, seq, d_model = x.shape
    depth = w_in.shape[0]
    n_heads = (w_in.shape[2] // 6) // HEAD_DIM
    n_tok = batch * seq
    row = lambda v: v.reshape(1, -1)

    h = x.reshape(n_tok, d_model)
    for l in range(depth):
        h, (w_in_b, w_out_b) = _ffn(
            h, row(ffn1_norm[l]), ffn1_w_gate[l].astype(BF16), ffn1_w_up[l].astype(BF16),
            ffn1_w_down[l].astype(BF16), tm=1024, tf=512, cast_along=(w_in[l], w_out[l]))
        q, k, v, yc = _in_proj(h, row(mix_norm[l]), w_in_b,
                               row(q_norm[l]) * (1.0 / math.sqrt(HEAD_DIM)),
                               row(k_norm[l]), conv_w[l],
                               row(conv_b[l]), row(conv_out_norm[l]), tm=512, seq=seq)
        ya = _attention(q, k, v, n_sub=32, head_rows=48)
        h, (wg2, wu2, wd2) = _out_proj(
            h, ya, yc, row(attn_out_norm[l]), w_out_b, tm=512,
            cast_along=(ffn2_w_gate[l], ffn2_w_up[l], ffn2_w_down[l]))
        h, _ = _ffn(h, row(ffn2_norm[l]), wg2, wu2, wd2, tm=1024, tf=512)
    return h.reshape(batch, seq, d_model)
```

```python
import functools
import math

import jax
import jax.numpy as jnp
from jax import lax
from jax.experimental import pallas as pl
from jax.experimental.pallas import tpu as pltpu

F32 = jnp.float32
BF16 = jnp.bfloat16

EPS = 1e-6
LOG2_E = math.log2(math.e)
FFN_RES_WEIGHT = 0.5
HEAD_DIM = 128
F32_SUBLANES = 8
BF16_SUBLANES = 16
CONV_WIDTH = 3
KEY_BLOCK = 128
CONV_HALO_ROWS = F32_SUBLANES

STICK_EXHAUSTED = 110.0

V7X_VMEM_LIMIT_BYTES = 56 * 1024 * 1024


def _rms_norm(x, gain):
    ms = jnp.mean(x * x, axis=-1, keepdims=True)
    return x * lax.rsqrt(ms + EPS) * gain


def _ffn_kernel(x_hbm, g_ref, wg_ref, wu_ref, wd_ref, *refs, n_cast):
    cast_src, (o_ref, *cast_dst) = refs[:n_cast], refs[n_cast:2 * n_cast + 1]
    x_buf, xn_ref, x_sem = refs[2 * n_cast + 1:]
    i = pl.program_id(0)
    j = pl.program_id(1)
    tm = o_ref.shape[0]

    def x_copy(tile):
        rows = pl.ds(pl.multiple_of(tile * tm, tm), tm)
        return pltpu.make_async_copy(x_hbm.at[rows, :], x_buf, x_sem)

    @pl.when(j == 0)
    def _():
        @pl.when(i == 0)
        def _():
            x_copy(0).start()

        x_copy(i).wait()
        x = x_buf[...]
        xn_ref[...] = _rms_norm(x, g_ref[...]).astype(BF16)
        o_ref[...] = x

        @pl.when(i + 1 < pl.num_programs(0))
        def _():
            x_copy(i + 1).start()

    xn = xn_ref[...]
    half = wg_ref.shape[1] // 2
    acts = []
    for c in (slice(0, half), slice(half, 2 * half)):
        hg = jnp.dot(xn, wg_ref[:, c], preferred_element_type=F32)
        hu = jnp.dot(xn, wu_ref[:, c], preferred_element_type=F32)
        acts.append((hg * jax.nn.sigmoid(hg) * hu * FFN_RES_WEIGHT).astype(BF16))
    o_ref[...] += jnp.dot(jnp.concatenate(acts, axis=1), wd_ref[...],
                          preferred_element_type=F32)

    for src, dst in zip(cast_src, cast_dst):
        dst[...] = src[...].astype(BF16)


def _ffn(x, gain, wg, wu, wd, *, tm, tf, cast_along=()):
    n_tok, d_model = x.shape
    d_ff = wg.shape[1]
    n_tiles, n_ff = n_tok // tm, d_ff // tf
    n_steps = n_tiles * n_ff

    def row_blocks(w):
        rows, cols = w.shape
        blk = BF16_SUBLANES * pl.cdiv(rows, BF16_SUBLANES * n_steps)
        assert rows % blk == 0
        last = rows // blk - 1
        return pl.BlockSpec((blk, cols), lambda i, j: (jnp.minimum(i * n_ff + j, last), 0))

    col_tile = pl.BlockSpec((d_model, tf), lambda i, j: (0, j))
    cast_specs = [row_blocks(w) for w in cast_along]
    out, *casts = pl.pallas_call(
        functools.partial(_ffn_kernel, n_cast=len(cast_along)),
        grid=(n_tiles, n_ff),
        in_specs=[
            pl.BlockSpec(memory_space=pl.ANY),
            pl.BlockSpec((1, d_model), lambda i, j: (0, 0)),
            col_tile, col_tile,
            pl.BlockSpec((tf, d_model), lambda i, j: (j, 0)),
            *cast_specs,
        ],
        out_specs=[pl.BlockSpec((tm, d_model), lambda i, j: (i, 0)), *cast_specs],
        out_shape=[jax.ShapeDtypeStruct((n_tok, d_model), F32),
                   *(jax.ShapeDtypeStruct(w.shape, BF16) for w in cast_along)],
        scratch_shapes=[pltpu.VMEM((tm, d_model), F32),
                        pltpu.VMEM((tm, d_model), BF16),
                        pltpu.SemaphoreType.DMA(())],
        compiler_params=pltpu.CompilerParams(
            dimension_semantics=("arbitrary", "arbitrary"),
            vmem_limit_bytes=V7X_VMEM_LIMIT_BYTES),
        name="ffn",
    )(x, gain, wg, wu, wd, *cast_along)
    return out, tuple(casts)


def _in_proj_kernel(x_ref, g_ref, w_ref, qg_ref, kg_ref, cw_ref, cb_ref, cg_ref,
                    q_ref, k_ref, v_ref, yc_ref, cbuf_ref, *, tiles_per_seq):
    tm = x_ref.shape[0]
    d_sec = yc_ref.shape[1]
    at_seq_start = pl.program_id(0) % tiles_per_seq == 0

    @pl.when(at_seq_start)
    def _():
        cbuf_ref[0:CONV_HALO_ROWS, :] = jnp.zeros((CONV_HALO_ROWS, d_sec), F32)

    @pl.when(jnp.logical_not(at_seq_start))
    def _():
        cbuf_ref[0:CONV_HALO_ROWS, :] = cbuf_ref[tm:tm + CONV_HALO_ROWS, :]

    xn = _rms_norm(x_ref[...], g_ref[...]).astype(BF16)

    def section(s):
        return jnp.dot(xn, w_ref[:, s * d_sec:(s + 1) * d_sec],
                       preferred_element_type=F32)

    def to_heads(a, out_ref, gain=None):
        for h in range(d_sec // HEAD_DIM):
            a_h = a[:, h * HEAD_DIM:(h + 1) * HEAD_DIM]
            out_ref[0, h] = (a_h if gain is None else _rms_norm(a_h, gain)).astype(BF16)

    cbuf_ref[CONV_HALO_ROWS:, :] = section(4) * section(5)
    y = cb_ref[...]
    for i in range(CONV_WIDTH):
        off = CONV_HALO_ROWS - (CONV_WIDTH - 1) + i
        y = y + cbuf_ref[off:off + tm, :] * cw_ref[i:i + 1, :]
    yc_ref[...] = _rms_norm(section(3) * y, cg_ref[...]).astype(BF16)

    to_heads(section(0), q_ref, qg_ref[...])
    to_heads(section(1), k_ref, kg_ref[...])
    to_heads(section(2), v_ref)


def _in_proj(x, gain, w_in, q_gain, k_gain, conv_w, conv_b, conv_gain, *, tm, seq):
    n_tok, d_model = x.shape
    d_sec = w_in.shape[1] // 6
    n_heads, tiles_per_seq = d_sec // HEAD_DIM, seq // tm
    sec_out = pl.BlockSpec((tm, d_sec), lambda i: (i, 0))
    out_sds = jax.ShapeDtypeStruct((n_tok, d_sec), BF16)
    head_out = pl.BlockSpec((1, n_heads, tm, HEAD_DIM),
                            lambda i: (i // tiles_per_seq, 0, i % tiles_per_seq, 0))
    head_sds = jax.ShapeDtypeStruct((n_tok // seq, n_heads, seq, HEAD_DIM), BF16)
    head_vec = pl.BlockSpec((1, HEAD_DIM), lambda i: (0, 0))
    sec_vec = pl.BlockSpec((1, d_sec), lambda i: (0, 0))
    return pl.pallas_call(
        functools.partial(_in_proj_kernel, tiles_per_seq=tiles_per_seq),
        grid=(n_tok // tm,),
        in_specs=[
            pl.BlockSpec((tm, d_model), lambda i: (i, 0)),
            pl.BlockSpec((1, d_model), lambda i: (0, 0)),
            pl.BlockSpec(w_in.shape, lambda i: (0, 0), pipeline_mode=pl.Buffered(1)),
            head_vec, head_vec,
            pl.BlockSpec((CONV_WIDTH, d_sec), lambda i: (0, 0)),
            sec_vec, sec_vec,
        ],
        out_specs=[head_out] * 3 + [sec_out],
        out_shape=[head_sds] * 3 + [out_sds],
        scratch_shapes=[pltpu.VMEM((tm + CONV_HALO_ROWS, d_sec), F32)],
        compiler_params=pltpu.CompilerParams(
            dimension_semantics=("arbitrary",),
            vmem_limit_bytes=V7X_VMEM_LIMIT_BYTES),
        name="in_proj",
    )(x, gain, w_in, q_gain, k_gain, conv_w, conv_b, conv_gain)


def _attn_kernel(q_ref, k_ref, v_ref, tri_ref, o_ref, acc_ref, c_ref,
                 *, n_sub, head_rows):
    qi = pl.program_id(2)
    first = qi * n_sub
    acc_ref[...] = jnp.zeros_like(acc_ref)

    row = lax.broadcasted_iota(jnp.int32, (KEY_BLOCK, KEY_BLOCK), 0)
    col = lax.broadcasted_iota(jnp.int32, (KEY_BLOCK, KEY_BLOCK), 1)
    strictly_causal = col < row

    def rows_of(m, lo, hi):
        return slice(m * KEY_BLOCK + lo, m * KEY_BLOCK + hi)

    def scores(r, lo=0, hi=KEY_BLOCK):
        on_diagonal = isinstance(r, int) and r == 0
        starts, live, log_beta, lhs = [], [], [], []
        for m in range(n_sub):
            kb = first + m - r
            live.append(strictly_causal[lo:hi] if on_diagonal else kb >= 0)
            start = pl.multiple_of(jnp.maximum(kb, 0) * KEY_BLOCK, KEY_BLOCK)
            starts.append(start)
            z = lax.dot_general(q_ref[0, 0, rows_of(m, lo, hi), :],
                                k_ref[0, 0, pl.ds(start, KEY_BLOCK), :],
                                (((1,), (1,)), ((), ())),
                                preferred_element_type=F32)
            sp = jnp.maximum(z, 0.0) + jnp.log(1.0 + jnp.exp2(jnp.abs(z) * -LOG2_E))
            log_beta.append(z - sp)
            sp_sum = jnp.where(live[m], sp, 0.0) if on_diagonal else sp
            hi_part = sp_sum.astype(BF16)
            lo_part = (sp_sum - hi_part.astype(F32)).astype(BF16)
            lhs.append(jnp.concatenate([hi_part, lo_part], axis=1))
        sums = jnp.dot(jnp.concatenate(lhs, axis=0), tri_ref[...],
                       preferred_element_type=F32)
        return starts, live, log_beta, sums, (lo, hi)

    def values(stage, c):
        starts, live, log_beta, sums, (lo, hi) = stage
        n = hi - lo
        whole = n == KEY_BLOCK
        c_rows = c if whole else jnp.concatenate(
            [c[rows_of(m, lo, hi), :] for m in range(n_sub)], axis=0)
        tail = sums[:, :KEY_BLOCK] + c_rows
        c_rows = c_rows + sums[:, KEY_BLOCK:]
        for m in range(n_sub):
            w = jnp.exp(log_beta[m] - tail[m * n:(m + 1) * n, :])
            v_blk = v_ref[0, 0, pl.ds(starts[m], KEY_BLOCK), :]
            if live[m].ndim == 0:
                v_blk = jnp.where(live[m], v_blk, jnp.zeros_like(v_blk))
            else:
                w = jnp.where(live[m], w, 0.0)
            acc_ref[rows_of(m, lo, hi), :] += jnp.dot(w.astype(BF16), v_blk,
                                                      preferred_element_type=F32)
        if whole:
            return c_rows
        pieces = []
        for m in range(n_sub):
            pieces += [c[rows_of(m, 0, lo), :], c_rows[m * n:(m + 1) * n, :],
                       c[rows_of(m, hi, KEY_BLOCK), :]]
        return jnp.concatenate([p for p in pieces if p.shape[0]], axis=0)

    def min_mass(c, r, lo=0, hi=KEY_BLOCK):
        smallest = None
        for m in range(n_sub):
            sub = c[rows_of(m, lo, hi), :].reshape(-1, F32_SUBLANES, KEY_BLOCK)
            sub = jnp.where(first + m - r >= 0, jnp.min(sub, axis=0), jnp.inf)
            smallest = sub if smallest is None else jnp.minimum(smallest, sub)
        return jnp.min(smallest)

    stages = [scores(0), scores(1), scores(2, 0, head_rows)]
    c = values(stages[1], values(stages[0], jnp.zeros(c_ref.shape, F32)))
    tail_rows_left = min_mass(c, 2, head_rows, KEY_BLOCK) <= STICK_EXHAUSTED
    c_ref[...] = values(stages[2], c)

    @pl.when(tail_rows_left)
    def _():
        c_ref[...] = values(scores(2, head_rows, KEY_BLOCK), c_ref[...])

    def keep_walking(carry):
        r, smallest = carry
        return jnp.logical_and(r <= first + n_sub - 1, smallest <= STICK_EXHAUSTED)

    def walk(carry):
        r, _ = carry
        c_new = values(scores(r), c_ref[...])
        c_ref[...] = c_new
        return r + 1, min_mass(c_new, r)

    lax.while_loop(keep_walking, walk, (jnp.int32(3), min_mass(c_ref[...], 2)))
    o_ref[0, 0] = acc_ref[...].astype(o_ref.dtype)


def _suffix_sum_matrix():
    j = lax.broadcasted_iota(jnp.int32, (2 * KEY_BLOCK, 2 * KEY_BLOCK), 0) % KEY_BLOCK
    s = lax.broadcasted_iota(jnp.int32, (2 * KEY_BLOCK, 2 * KEY_BLOCK), 1)
    return jnp.where(jnp.logical_or(s >= KEY_BLOCK, j > s), 1.0, 0.0).astype(BF16)


def _attention(q, k, v, *, n_sub, head_rows):
    batch, n_heads, seq, _ = q.shape
    tq = n_sub * KEY_BLOCK
    q_spec = pl.BlockSpec((1, 1, tq, HEAD_DIM), lambda b, h, i: (b, h, i, 0))
    kv_spec = pl.BlockSpec((1, 1, seq, HEAD_DIM), lambda b, h, i: (b, h, 0, 0))
    return pl.pallas_call(
        functools.partial(_attn_kernel, n_sub=n_sub, head_rows=head_rows),
        grid=(batch, n_heads, seq // tq),
        in_specs=[
            q_spec, kv_spec, kv_spec,
            pl.BlockSpec((2 * KEY_BLOCK, 2 * KEY_BLOCK), lambda b, h, i: (0, 0)),
        ],
        out_specs=q_spec,
        out_shape=jax.ShapeDtypeStruct(q.shape, BF16),
        scratch_shapes=[pltpu.VMEM((tq, HEAD_DIM), F32),
                        pltpu.VMEM((tq, KEY_BLOCK), F32)],
        compiler_params=pltpu.CompilerParams(
            dimension_semantics=("parallel", "parallel", "arbitrary"),
            vmem_limit_bytes=V7X_VMEM_LIMIT_BYTES),
        name="attention",
    )(q, k, v, _suffix_sum_matrix())


def _out_proj_kernel(x_ref, ya_ref, yc_ref, ag_ref, w_ref, o_ref):
    ya = jnp.concatenate([ya_ref[0, h] for h in range(ya_ref.shape[1])], axis=1)
    y_attn = _rms_norm(ya.astype(F32), ag_ref[...]).astype(BF16)
    mix = jnp.concatenate([y_attn, yc_ref[...]], axis=1)
    o_ref[...] = x_ref[...] + jnp.dot(mix, w_ref[...], preferred_element_type=F32)


def _out_proj(x, ya, yc, attn_gain, w_out, *, tm):
    n_tok, d_model = x.shape
    _, n_heads, seq, _ = ya.shape
    d_sec = yc.shape[1]
    tiles_per_seq = seq // tm
    sec = pl.BlockSpec((tm, d_sec), lambda i: (i, 0))
    return pl.pallas_call(
        _out_proj_kernel,
        grid=(n_tok // tm,),
        in_specs=[
            pl.BlockSpec((tm, d_model), lambda i: (i, 0)),
            pl.BlockSpec((1, n_heads, tm, HEAD_DIM),
                         lambda i: (i // tiles_per_seq, 0, i % tiles_per_seq, 0)),
            sec,
            pl.BlockSpec((1, d_sec), lambda i: (0, 0)),
            pl.BlockSpec((d_model, d_model), lambda i: (0, 0)),
        ],
        out_specs=pl.BlockSpec((tm, d_model), lambda i: (i, 0)),
        out_shape=jax.ShapeDtypeStruct((n_tok, d_model), F32),
        compiler_params=pltpu.CompilerParams(
            dimension_semantics=("parallel",),
            vmem_limit_bytes=V7X_VMEM_LIMIT_BYTES),
        name="out_proj",
    )(x, ya, yc, attn_gain, w_out)


def kernel(x, ffn1_norm, ffn1_w_gate, ffn1_w_up, ffn1_w_down, mix_norm, w_in, q_norm, k_norm, conv_w, conv_b, attn_out_norm, conv_out_norm, w_out, ffn2_norm, ffn2_w_gate, ffn2_w_up, ffn2_w_down):
    batch, seq, d_model = x.shape
    depth = w_in.shape[0]
    n_tok = batch * seq
    row = lambda v: v.reshape(1, -1)

    h = x.reshape(n_tok, d_model)
    for l in range(depth):
        h, (w_in_b, w_out_b, wg2, wu2, wd2) = _ffn(
            h, row(ffn1_norm[l]), ffn1_w_gate[l].astype(BF16), ffn1_w_up[l].astype(BF16),
            ffn1_w_down[l].astype(BF16), tm=1024, tf=512,
            cast_along=(w_in[l], w_out[l], ffn2_w_gate[l], ffn2_w_up[l], ffn2_w_down[l]))
        q, k, v, yc = _in_proj(h, row(mix_norm[l]), w_in_b,
                               row(q_norm[l]) * (1.0 / math.sqrt(HEAD_DIM)),
                               row(k_norm[l]), conv_w[l],
                               row(conv_b[l]), row(conv_out_norm[l]), tm=512, seq=seq)
        ya = _attention(q, k, v, n_sub=32, head_rows=48)
        h = _out_proj(h, ya, yc, row(attn_out_norm[l]), w_out_b, tm=512)
        h, _ = _ffn(h, row(ffn2_norm[l]), wg2, wu2, wd2, tm=1024, tf=512)
    return h.reshape(batch, seq, d_model)
```
